```python
import jax
import jax.numpy as jnp
from jax import lax
import numpy as np

D_MODEL = 4096
BATCH = 4
SEQ = 4096
DEPTH = 4

GRID_W = 64
CTX_LEN = 256
HEAD_DIM = 128
N_MIX_HEADS = D_MODEL // HEAD_DIM
A_HEADS = 3 * N_MIX_HEADS // 8
A_KV_HEADS = A_HEADS // 3
C_HEADS = 3 * N_MIX_HEADS // 8
B_CH = D_MODEL - (A_HEADS + C_HEADS) * HEAD_DIM
A_Q = A_HEADS * HEAD_DIM
A_KV = A_KV_HEADS * HEAD_DIM
C_W = C_HEADS * HEAD_DIM
IN_W = A_Q + 2 * A_KV + 2 * B_CH + 3 * C_W
IN_SPLITS = (A_Q, A_Q + A_KV, A_Q + 2 * A_KV, A_Q + 2 * A_KV + 2 * B_CH,
             A_Q + 2 * A_KV + 2 * B_CH + C_W, A_Q + 2 * A_KV + 2 * B_CH + 2 * C_W)
B_CONV = 31
NA_ROWS = 8
NA_COLS = 16
D_FF = 11 * D_MODEL // 8
FFN_CONV = 3
Q_BLOCK = 128
ROPE_THETA = 10000.0
EPS = 1e-6
N_MOD = 6
NEG_INF = -1e30

kernel_name = 'hybrid_gqa_conformer_natten_dit'


def rms_norm(x, g):
    xf = x.astype(jnp.float32)
    y = xf * lax.rsqrt(jnp.mean(xf * xf, axis=-1, keepdims=True) + EPS)
    return (y * g.astype(jnp.float32)).astype(x.dtype)


def layer_norm(x, g, b):
    xf = x.astype(jnp.float32)
    mu = jnp.mean(xf, axis=-1, keepdims=True)
    var = jnp.mean(jnp.square(xf - mu), axis=-1, keepdims=True)
    y = (xf - mu) * lax.rsqrt(var + EPS)
    return (y * g.astype(jnp.float32) + b.astype(jnp.float32)).astype(x.dtype)


def modulate(h, shift, scale):
    return h * (1 + scale) + shift


def heads(t, n):
    return t.reshape(t.shape[:-1] + (n, HEAD_DIM))


def axial_rope_tables(n_tokens):
    t = jnp.arange(n_tokens, dtype=jnp.int32)
    row = (t // GRID_W).astype(jnp.float32)
    col = (t % GRID_W).astype(jnp.float32)
    half = HEAD_DIM // 2
    inv = ROPE_THETA ** (-jnp.arange(0, half, 2, dtype=jnp.float32) / half)
    ang = jnp.concatenate([row[:, None] * inv, col[:, None] * inv], axis=-1)
    return jnp.cos(ang), jnp.sin(ang)


def apply_rope(x, cos, sin):
    xf = x.astype(jnp.float32)
    x1, x2 = xf[..., 0::2], xf[..., 1::2]
    c, s = cos[None, :, None, :], sin[None, :, None, :]
    out = jnp.stack([x1 * c - x2 * s, x1 * s + x2 * c], axis=-1).reshape(x.shape)
    return out.astype(x.dtype)


def depthwise_conv(x, w, b):
    y = lax.conv_general_dilated(x, w[:, None, :], window_strides=(1,), padding='SAME',
                                 dimension_numbers=('NWC', 'WIO', 'NWC'),
                                 feature_group_count=x.shape[-1])
    return y + b


def gqa_attend(q, k, v):
    B, Lq, H, Dh = q.shape
    Hkv = k.shape[2]
    qg = q.reshape(B, Lq, Hkv, H // Hkv, Dh)
    s = jnp.einsum('bqkgd,bskd->bkgqs', qg, k).astype(jnp.float32) * (Dh ** -0.5)
    p = jax.nn.softmax(s, axis=-1).astype(v.dtype)
    return jnp.einsum('bkgqs,bskd->bqkgd', p, v).reshape(B, Lq, H * Dh)


def gqa_blocked(q, k, v):
    B, S, H, Dh = q.shape
    nblk = S // Q_BLOCK
    qb = jnp.swapaxes(q.reshape(B, nblk, Q_BLOCK, H, Dh), 0, 1)
    o = lax.map(lambda qq: gqa_attend(qq, k, v), qb)
    return jnp.swapaxes(o, 0, 1).reshape(B, S, H * Dh)


def neighbourhood_attention(q, k, v, k_ctx, v_ctx, rpb):
    B, S, H, Dh = q.shape
    rows = S // GRID_W
    kh = min(NA_ROWS, rows)
    scale = Dh ** -0.5
    qg = jnp.moveaxis(q.reshape(B, rows, GRID_W, H, Dh), 1, 0)
    kg = k.reshape(B, rows, GRID_W, H, Dh)
    vg = v.reshape(B, rows, GRID_W, H, Dh)
    col = jnp.arange(GRID_W, dtype=jnp.int32)
    col_start = jnp.clip(col - NA_COLS // 2, 0, GRID_W - NA_COLS)
    col_mask = (col[None, :] >= col_start[:, None]) & (col[None, :] < col_start[:, None] + NA_COLS)
    col_idx = jnp.clip(col[None, :] - col[:, None] + NA_COLS - 1, 0, 2 * NA_COLS - 2)
    row = jnp.arange(rows, dtype=jnp.int32)
    row_start = jnp.clip(row - kh // 2, 0, rows - kh)
    n_nb = kh * GRID_W

    def one_row(args):
        q_row, r, r0 = args
        k_rows = lax.dynamic_slice_in_dim(kg, r0, kh, axis=1)
        v_rows = lax.dynamic_slice_in_dim(vg, r0, kh, axis=1)
        row_idx = r0 + jnp.arange(kh, dtype=jnp.int32) - r + NA_ROWS - 1
        bias = rpb[:, row_idx[:, None, None], col_idx[None, :, :]]
        bias = jnp.transpose(bias, (0, 2, 1, 3)).astype(jnp.float32)
        s_nb = jnp.einsum('bqhd,bjkhd->bhqjk', q_row, k_rows).astype(jnp.float32) * scale + bias[None]
        s_nb = jnp.where(col_mask[:, None, :], s_nb, NEG_INF)
        s_cx = jnp.einsum('bqhd,bchd->bhqc', q_row, k_ctx).astype(jnp.float32) * scale
        s_all = jnp.concatenate([s_nb.reshape(B, H, GRID_W, n_nb), s_cx], axis=-1)
        p = jax.nn.softmax(s_all, axis=-1).astype(v.dtype)
        p_nb = p[..., :n_nb].reshape(B, H, GRID_W, kh, GRID_W)
        return (jnp.einsum('bhqjk,bjkhd->bqhd', p_nb, v_rows)
                + jnp.einsum('bhqc,bchd->bqhd', p[..., n_nb:], v_ctx))

    o = lax.map(one_row, (qg, row, row_start))
    return jnp.moveaxis(o, 0, 1).reshape(B, S, H * Dh)


def conformer_conv(u, dw_w, dw_b, ln_g, ln_b, pw_w, pw_b):
    a, g = jnp.split(u, 2, axis=-1)
    z = a * jax.nn.sigmoid(g)
    z = depthwise_conv(z, dw_w, dw_b)
    z = jax.nn.silu(layer_norm(z, ln_g, ln_b))
    return z @ pw_w + pw_b


def conv_ffn(h, w_up, dw_w, dw_b, w_down):
    u = depthwise_conv(h @ w_up, dw_w, dw_b)
    gate, val = jnp.split(u, 2, axis=-1)
    return (jax.nn.silu(gate) * val) @ w_down


def setup_inputs(seed: int = 0) -> dict:
    key = jax.random.key(seed)
    keys = iter(jax.random.split(key, 32))

    def nrm(shape, scale):
        return jax.random.normal(next(keys), shape, jnp.float32) * scale

    L, D = DEPTH, D_MODEL
    return {
        'x': nrm((BATCH, SEQ, D), 1.0),
        'c': nrm((BATCH, D), 1.0),
        'ctx': nrm((BATCH, CTX_LEN, D), 1.0),
        'c_ctx': nrm((D,), 1.0),
        'ada_w': nrm((L, D, N_MOD * D), 0.5 * D ** -0.5),
        'ada_b': nrm((L, N_MOD * D), 0.02),
        'norm1_g': 1.0 + nrm((L, D), 0.02),
        'norm2_g': 1.0 + nrm((L, D), 0.02),
        'w_in': nrm((L, D, IN_W), D ** -0.5),
        'a_qn_g': 1.0 + nrm((L, HEAD_DIM), 0.02),
        'a_kn_g': 1.0 + nrm((L, HEAD_DIM), 0.02),
        'b_dw_w': nrm((L, B_CONV, B_CH), B_CONV ** -0.5),
        'b_dw_b': nrm((L, B_CH), 0.02),
        'b_ln_g': 1.0 + nrm((L, B_CH), 0.02),
        'b_ln_b': nrm((L, B_CH), 0.02),
        'b_pw_w': nrm((L, B_CH, B_CH), B_CH ** -0.5),
        'b_pw_b': nrm((L, B_CH), 0.02),
        'c_rpb': nrm((L, C_HEADS, 2 * NA_ROWS - 1, 2 * NA_COLS - 1), 0.1),
        'w_out': nrm((L, D, D), D ** -0.5),
        'ffn_w_up': nrm((L, D, 2 * D_FF), D ** -0.5),
        'ffn_dw_w': nrm((L, FFN_CONV, 2 * D_FF), FFN_CONV ** -0.5),
        'ffn_dw_b': nrm((L, 2 * D_FF), 0.02),
        'ffn_w_down': nrm((L, D_FF, D), D_FF ** -0.5),
        'final_g': 1.0 + nrm((D,), 0.02),
    }


def reference(x, c, ctx, c_ctx, ada_w, ada_b, norm1_g, norm2_g, w_in, a_qn_g, a_kn_g,
              b_dw_w, b_dw_b, b_ln_g, b_ln_b, b_pw_w, b_pw_b, c_rpb, w_out,
              ffn_w_up, ffn_dw_w, ffn_dw_b, ffn_w_down, final_g):
    S = x.shape[1]
    cos, sin = axial_rope_tables(S)
    silu_c = jax.nn.silu(c)
    silu_cc = jax.nn.silu(c_ctx)
    for l in range(DEPTH):
        ctx_needed = l < DEPTH - 1
        mod = (silu_c @ ada_w[l] + ada_b[l])[:, None, :]
        mod_c = silu_cc @ ada_w[l] + ada_b[l]
        sh1, sc1, g1, sh2, sc2, g2 = jnp.split(mod, N_MOD, axis=-1)
        sh1c, sc1c, g1c, sh2c, sc2c, g2c = jnp.split(mod_c, N_MOD, axis=-1)

        h = modulate(rms_norm(x, norm1_g[l]), sh1, sc1)
        hc = modulate(rms_norm(ctx, norm1_g[l]), sh1c, sc1c)
        qa, ka, va, bu, qn, kn, vn = jnp.split(h @ w_in[l], IN_SPLITS, axis=-1)
        qa_c, ka_c, va_c, bu_c, qn_c, kn_c, vn_c = jnp.split(hc @ w_in[l], IN_SPLITS, axis=-1)

        qa = apply_rope(rms_norm(heads(qa, A_HEADS), a_qn_g[l]), cos, sin)
        ka = apply_rope(rms_norm(heads(ka, A_KV_HEADS), a_kn_g[l]), cos, sin)
        va = heads(va, A_KV_HEADS)
        ka_c = rms_norm(heads(ka_c, A_KV_HEADS), a_kn_g[l])
        va_c = heads(va_c, A_KV_HEADS)
        o_a = gqa_blocked(qa, jnp.concatenate([ka_c, ka], axis=1), jnp.concatenate([va_c, va], axis=1))

        o_b = conformer_conv(bu, b_dw_w[l], b_dw_b[l], b_ln_g[l], b_ln_b[l], b_pw_w[l], b_pw_b[l])

        kn_c_h, vn_c_h = heads(kn_c, C_HEADS), heads(vn_c, C_HEADS)
        o_c = neighbourhood_attention(heads(qn, C_HEADS), heads(kn, C_HEADS), heads(vn, C_HEADS),
                                      kn_c_h, vn_c_h, c_rpb[l])

        x = x + g1 * (jnp.concatenate([o_a, o_b, o_c], axis=-1) @ w_out[l])

        h2 = modulate(rms_norm(x, norm2_g[l]), sh2, sc2)
        x = x + g2 * conv_ffn(h2, ffn_w_up[l], ffn_dw_w[l], ffn_dw_b[l], ffn_w_down[l])

        if ctx_needed:
            qa_c = rms_norm(heads(qa_c, A_HEADS), a_qn_g[l])
            oc_a = gqa_attend(qa_c, ka_c, va_c)
            oc_b = conformer_conv(bu_c, b_dw_w[l], b_dw_b[l], b_ln_g[l], b_ln_b[l], b_pw_w[l], b_pw_b[l])
            oc_c = gqa_attend(heads(qn_c, C_HEADS), kn_c_h, vn_c_h)
            ctx = ctx + g1c * (jnp.concatenate([oc_a, oc_b, oc_c], axis=-1) @ w_out[l])
            h2c = modulate(rms_norm(ctx, norm2_g[l]), sh2c, sc2c)
            ctx = ctx + g2c * conv_ffn(h2c, ffn_w_up[l], ffn_dw_w[l], ffn_dw_b[l], ffn_w_down[l])

    return rms_norm(x, final_g)
```

```python
import functools

import jax
import jax.numpy as jnp
from jax import lax
from jax.experimental import pallas as pl
from jax.experimental.pallas import tpu as pltpu

HEAD_DIM = 128
GRID_W = 64
GRID_SHIFT = 6
NA_ROWS = 8
NA_COLS = 16
ROPE_THETA = 10000.0
EPS = 1e-6
N_MOD = 6
NEG_INF = -1e30

V7X_VMEM_LIMIT_BYTES = 56 * 1024 * 1024
MOD_ROWS = 8
BF16_ROW_TILE = 16
CONV_HALO = 16

f32 = jnp.float32
bf16 = jnp.bfloat16
_NT = (((1,), (1,)), ((), ()))


def _cp(*sem):
    return pltpu.CompilerParams(dimension_semantics=sem, vmem_limit_bytes=V7X_VMEM_LIMIT_BYTES)


def _tile(n, pref, unit):
    t = (min(pref, n) // unit) * unit
    while n % t:
        t -= unit
    return t


def _sigmoid(x):
    return 1.0 / (1.0 + jnp.exp(-x))


def _mod_kernel(c_ref, w_ref, b_ref, o_ref):
    c = c_ref[...]
    sc = (c * _sigmoid(c)).astype(bf16)
    o_ref[...] = jnp.dot(sc, w_ref[...].astype(bf16), preferred_element_type=f32) + b_ref[...]


def _ada_mod(cvec, ada_w, ada_b):
    L, D, N = ada_w.shape
    tn = _tile(N, 512, 128)
    return pl.pallas_call(
        _mod_kernel,
        out_shape=jax.ShapeDtypeStruct((L, MOD_ROWS, N), f32),
        grid=(L, N // tn),
        in_specs=[pl.BlockSpec((MOD_ROWS, D), lambda l, j: (0, 0)),
                  pl.BlockSpec((None, D, tn), lambda l, j: (l, 0, j)),
                  pl.BlockSpec((None, 1, tn), lambda l, j: (l, 0, j))],
        out_specs=pl.BlockSpec((None, MOD_ROWS, tn), lambda l, j: (l, 0, j)),
        compiler_params=_cp("arbitrary", "arbitrary"),
        name="ada_mod",
    )(cvec, ada_w, ada_b.reshape(L, 1, N))


def _norm_mod_kernel(x_ref, g_ref, sh_ref, sc_ref, o_ref):
    x = x_ref[...]
    y = x * lax.rsqrt(jnp.mean(x * x, axis=-1, keepdims=True) + EPS) * g_ref[...]
    o_ref[...] = (y * (1.0 + sc_ref[...]) + sh_ref[...]).astype(o_ref.dtype)


def _norm_kernel(x_ref, g_ref, o_ref):
    x = x_ref[...]
    o_ref[...] = x * lax.rsqrt(jnp.mean(x * x, axis=-1, keepdims=True) + EPS) * g_ref[...]


def _mod_row(i, tm, n_lat, S, B):
    return jnp.where(i * tm < n_lat, (i * tm) // S, B)


def _norm_mod(X, g, sh, sc, *, n_lat, S, B, tm):
    T, D = X.shape
    mrow = lambda i: (_mod_row(i, tm, n_lat, S, B), 0, 0)
    return pl.pallas_call(
        _norm_mod_kernel,
        out_shape=jax.ShapeDtypeStruct((T, D), bf16),
        grid=(T // tm,),
        in_specs=[pl.BlockSpec((tm, D), lambda i: (i, 0)),
                  pl.BlockSpec((1, D), lambda i: (0, 0)),
                  pl.BlockSpec((None, 1, D), mrow),
                  pl.BlockSpec((None, 1, D), mrow)],
        out_specs=pl.BlockSpec((tm, D), lambda i: (i, 0)),
        compiler_params=_cp("arbitrary"),
        name="norm_mod",
    )(X, g.reshape(1, D), sh, sc)


def _final_norm(X, g, *, rows, tm):
    D = X.shape[1]
    return pl.pallas_call(
        _norm_kernel,
        out_shape=jax.ShapeDtypeStruct((rows, D), f32),
        grid=(rows // tm,),
        in_specs=[pl.BlockSpec((tm, D), lambda i: (i, 0)),
                  pl.BlockSpec((1, D), lambda i: (0, 0))],
        out_specs=pl.BlockSpec((tm, D), lambda i: (i, 0)),
        compiler_params=_cp("arbitrary"),
        name="final_norm",
    )(X, g.reshape(1, D))


def _mm_kernel(h_ref, w_ref, o_ref):
    o_ref[...] = jnp.dot(h_ref[...], w_ref[...], preferred_element_type=f32).astype(o_ref.dtype)


def _proj(h, w, col0, ncols, out_dtype, *, tm, tn_pref, name):
    T, D = h.shape
    tn = _tile(ncols, tn_pref, 128)
    while col0 % tn or ncols % tn:
        tn -= 128
    c0 = col0 // tn
    return pl.pallas_call(
        _mm_kernel,
        out_shape=jax.ShapeDtypeStruct((T, ncols), out_dtype),
        grid=(T // tm, ncols // tn),
        in_specs=[pl.BlockSpec((tm, D), lambda i, j: (i, 0)),
                  pl.BlockSpec((D, tn), lambda i, j: (0, c0 + j))],
        out_specs=pl.BlockSpec((tm, tn), lambda i, j: (i, j)),
        compiler_params=_cp("arbitrary", "arbitrary"),
        name=name,
    )(h, w)


def _mm_qk_kernel(h_ref, w_ref, g_ref, cos_ref, sin_ref, o_ref):
    acc = jnp.dot(h_ref[...], w_ref[...], preferred_element_type=f32)
    tm, tn = acc.shape
    even = (lax.broadcasted_iota(jnp.int32, (tm, HEAD_DIM), 1) & 1) == 0
    cosv = cos_ref[...]
    sinv = sin_ref[...]
    for hh in range(tn // HEAD_DIM):
        sl = slice(hh * HEAD_DIM, (hh + 1) * HEAD_DIM)
        blk = acc[:, sl]
        y = blk * lax.rsqrt(jnp.mean(blk * blk, axis=-1, keepdims=True) + EPS) * g_ref[:, sl]
        partner = jnp.where(even, pltpu.roll(y, HEAD_DIM - 1, 1), pltpu.roll(y, 1, 1))
        o_ref[:, sl] = (y * cosv + partner * sinv).astype(o_ref.dtype)


def _proj_qk(h, w, gains, cos_t, sin_t, ncols, *, tm, n_lat, S):
    T, D = h.shape
    tn = _tile(ncols, 1024, 128)
    pos_blocks = S // tm

    def pos(i, j):
        return (jnp.where(i * tm < n_lat, i % pos_blocks, pos_blocks), 0)

    return pl.pallas_call(
        _mm_qk_kernel,
        out_shape=jax.ShapeDtypeStruct((T, ncols), bf16),
        grid=(T // tm, ncols // tn),
        in_specs=[pl.BlockSpec((tm, D), lambda i, j: (i, 0)),
                  pl.BlockSpec((D, tn), lambda i, j: (0, j)),
                  pl.BlockSpec((1, tn), lambda i, j: (0, j)),
                  pl.BlockSpec((tm, HEAD_DIM), pos),
                  pl.BlockSpec((tm, HEAD_DIM), pos)],
        out_specs=pl.BlockSpec((tm, tn), lambda i, j: (i, j)),
        compiler_params=_cp("arbitrary", "arbitrary"),
        name="proj_qk",
    )(h, w, gains, cos_t, sin_t)


def _attn_kernel(*refs, n_seg, group, scale):
    q_ref = refs[0]
    kv = refs[1:1 + 2 * n_seg]
    o_ref = refs[1 + 2 * n_seg]
    for g in range(group):
        sl = slice(g * HEAD_DIM, (g + 1) * HEAD_DIM)
        q = q_ref[:, sl]
        ss = [lax.dot_general(q, kv[2 * i][...], _NT, preferred_element_type=f32) * scale
              for i in range(n_seg)]
        m = jnp.max(ss[0], axis=-1, keepdims=True)
        for s in ss[1:]:
            m = jnp.maximum(m, jnp.max(s, axis=-1, keepdims=True))
        l = None
        o = None
        for i, s in enumerate(ss):
            p = jnp.exp(s - m)
            li = jnp.sum(p, axis=-1, keepdims=True)
            oi = jnp.dot(p.astype(bf16), kv[2 * i + 1][...], preferred_element_type=f32)
            l = li if l is None else l + li
            o = oi if o is None else o + oi
        o_ref[:, sl] = (o / l).astype(o_ref.dtype)


def _attention(q_arr, q_col0, kv_segs, out_cols, *, nb, heads, group, q_row0, q_len, tq,
               prev_out=None, out_rows=None, name):
    scale = HEAD_DIM ** -0.5
    gw = group * HEAD_DIM
    n_kv = heads // group
    nq = q_len // tq
    qb0 = q_row0 // tq
    qc0 = q_col0 // gw
    in_specs = [pl.BlockSpec((tq, gw), lambda j, b, i: (qb0 + b * nq + i, qc0 + j))]
    args = [q_arr]
    for (k_arr, k_col0, v_arr, v_col0, row0, length) in kv_segs:
        rb0 = row0 // length
        kc0 = k_col0 // HEAD_DIM
        vc0 = v_col0 // HEAD_DIM
        in_specs.append(pl.BlockSpec((length, HEAD_DIM),
                                     lambda j, b, i, rb0=rb0, kc0=kc0: (rb0 + b, kc0 + j)))
        in_specs.append(pl.BlockSpec((length, HEAD_DIM),
                                     lambda j, b, i, rb0=rb0, vc0=vc0: (rb0 + b, vc0 + j)))
        args += [k_arr, v_arr]
    aliases = {}
    if prev_out is not None:
        in_specs.append(pl.BlockSpec(memory_space=pl.ANY))
        args.append(prev_out)
        aliases = {len(args) - 1: 0}
        out_rows = prev_out.shape[0]
    kern = functools.partial(_attn_kernel_aliased if prev_out is not None else _attn_kernel,
                             n_seg=len(kv_segs), group=group, scale=scale)
    return pl.pallas_call(
        kern,
        out_shape=jax.ShapeDtypeStruct((out_rows, out_cols), bf16),
        grid=(n_kv, nb, nq),
        in_specs=in_specs,
        out_specs=pl.BlockSpec((tq, gw), lambda j, b, i: (qb0 + b * nq + i, j)),
        input_output_aliases=aliases,
        compiler_params=_cp("arbitrary", "arbitrary", "arbitrary"),
        name=name,
    )(*args)


def _attn_kernel_aliased(*refs, n_seg, group, scale):
    n_in = 1 + 2 * n_seg
    _attn_kernel(*refs[:n_in], refs[n_in + 1], n_seg=n_seg, group=group, scale=scale)


def _na_bias_kernel(rpb_ref, o_ref, tw_ref, *, rows):
    h = pl.program_id(0)
    case = pl.program_id(1)
    n_r = 2 * NA_ROWS - 1
    n_c = 2 * NA_COLS - 1
    per_head = n_r * n_c
    shape = (GRID_W, 2 * GRID_W)
    qc = lax.broadcasted_iota(jnp.int32, shape, 0)
    lane = lax.broadcasted_iota(jnp.int32, shape, 1)
    kc = lane & (GRID_W - 1)
    second = lane >= GRID_W
    dc = kc - qc
    cs = jnp.clip(qc - NA_COLS // 2, 0, GRID_W - NA_COLS)
    col_ok = (kc >= cs) & (kc < cs + NA_COLS)
    neg = jnp.full(shape, NEG_INF, f32)
    lo, hi = -NA_ROWS, NA_ROWS - 1
    span = 2 * NA_ROWS - 1
    for a in range(-span, span):
        if lo <= a <= hi:
            acc = neg
            for bb in range(n_c):
                s0 = rpb_ref[h * per_head + (a + NA_ROWS - 1) * n_c + bb] if -NA_ROWS < a < NA_ROWS else NEG_INF
                s1 = rpb_ref[h * per_head + (a + NA_ROWS) * n_c + bb] if -NA_ROWS < a + 1 < NA_ROWS else NEG_INF
                acc = jnp.where(dc == bb - (NA_COLS - 1), jnp.where(second, s1, s0), acc)
            tw_ref[a + span] = jnp.where(col_ok, acc, NEG_INF)
        else:
            tw_ref[a + span] = neg
    qbase = jnp.where(case == 0, 0, jnp.where(case == 1, NA_ROWS, rows - NA_ROWS))
    kbase = jnp.where(case == 0, 0, jnp.where(case == 1, NA_ROWS // 2, rows - 2 * NA_ROWS))
    off = kbase - qbase
    for qr in range(NA_ROWS):
        for kp in range(NA_ROWS):
            a = 2 * kp - qr + off
            o_ref[qr * GRID_W:(qr + 1) * GRID_W, kp * 2 * GRID_W:(kp + 1) * 2 * GRID_W] = tw_ref[a + span]
    full = (NA_ROWS * GRID_W, 2 * NA_ROWS * GRID_W)
    qrow = qbase + (lax.broadcasted_iota(jnp.int32, full, 0) >> GRID_SHIFT)
    krow = kbase + (lax.broadcasted_iota(jnp.int32, full, 1) >> GRID_SHIFT)
    rs = jnp.clip(qrow - NA_ROWS // 2, 0, rows - NA_ROWS)
    valid = (krow >= rs) & (krow < rs + NA_ROWS)
    o_ref[...] = jnp.where(valid, o_ref[...], NEG_INF)


def _na_bias(rpb_l, *, rows):
    H = rpb_l.shape[0]
    tq, tk = NA_ROWS * GRID_W, 2 * NA_ROWS * GRID_W
    return pl.pallas_call(
        functools.partial(_na_bias_kernel, rows=rows),
        out_shape=jax.ShapeDtypeStruct((H, 3, tq, tk), f32),
        grid=(H, 3),
        in_specs=[pl.BlockSpec(memory_space=pltpu.SMEM)],
        out_specs=pl.BlockSpec((None, None, tq, tk), lambda h, c: (h, c, 0, 0)),
        scratch_shapes=[pltpu.VMEM((2 * (2 * NA_ROWS - 1), GRID_W, 2 * GRID_W), f32)],
        compiler_params=_cp("arbitrary", "arbitrary"),
        name="na_bias",
    )(rpb_l.reshape(-1))


def _na_kernel(q_ref, k0, k1, k2, k3, v0, v1, v2, v3, kc_ref, vc_ref, bias_ref, o_ref, *, scale):
    q = q_ref[...]
    ks = (k0, k1, k2, k3)
    vs = (v0, v1, v2, v3)
    kb = k0.shape[0]
    ss = [lax.dot_general(q, ks[i][...], _NT, preferred_element_type=f32) * scale
          + bias_ref[:, i * kb:(i + 1) * kb] for i in range(4)]
    ss.append(lax.dot_general(q, kc_ref[...], _NT, preferred_element_type=f32) * scale)
    vals = vs + (vc_ref,)
    m = jnp.max(ss[0], axis=-1, keepdims=True)
    for s in ss[1:]:
        m = jnp.maximum(m, jnp.max(s, axis=-1, keepdims=True))
    l = None
    o = None
    for s, v_ref in zip(ss, vals):
        p = jnp.exp(s - m)
        li = jnp.sum(p, axis=-1, keepdims=True)
        oi = jnp.dot(p.astype(bf16), v_ref[...], preferred_element_type=f32)
        l = li if l is None else l + li
        o = oi if o is None else o + oi
    o_ref[...] = (o / l).astype(o_ref.dtype)


def _na_attention(nqkv, bias, *, B, S, CTX, heads, out_rows):
    rows = S // GRID_W
    tq = NA_ROWS * GRID_W
    kb = tq // 2
    nt = rows // NA_ROWS
    kb_per_seq = S // kb
    ctx_b0 = (B * S) // CTX

    def case(t):
        return jnp.where(t == 0, 0, jnp.where(t == nt - 1, 2, 1))

    def kblock(t):
        return jnp.clip(2 * t - 1, 0, kb_per_seq - 4)

    def kspec(col0, jj):
        return pl.BlockSpec((kb, HEAD_DIM),
                            lambda h, t, b: (b * kb_per_seq + kblock(t) + jj, col0 + h))

    in_specs = [pl.BlockSpec((tq, HEAD_DIM), lambda h, t, b: (b * nt + t, h))]
    in_specs += [kspec(heads, jj) for jj in range(4)]
    in_specs += [kspec(2 * heads, jj) for jj in range(4)]
    in_specs += [pl.BlockSpec((CTX, HEAD_DIM), lambda h, t, b: (ctx_b0 + b, heads + h)),
                 pl.BlockSpec((CTX, HEAD_DIM), lambda h, t, b: (ctx_b0 + b, 2 * heads + h)),
                 pl.BlockSpec((None, None, tq, 2 * tq), lambda h, t, b: (h, case(t), 0, 0))]
    return pl.pallas_call(
        functools.partial(_na_kernel, scale=HEAD_DIM ** -0.5),
        out_shape=jax.ShapeDtypeStruct((out_rows, heads * HEAD_DIM), bf16),
        grid=(heads, nt, B),
        in_specs=in_specs,
        out_specs=pl.BlockSpec((tq, HEAD_DIM), lambda h, t, b: (b * nt + t, h)),
        compiler_params=_cp("arbitrary", "arbitrary", "arbitrary"),
        name="na_attn",
    )(*([nqkv] * 11), bias)


def _conf_kernel(prev_ref, cur_ref, next_ref, dww_ref, dwb_ref, lng_ref, lnb_ref, pww_ref, pwb_ref,
                 o_ref, z_ref, y_ref, *, tm, ch, n_taps, lat_tiles, lat_tiles_per_seq, ctx_tiles_per_seq):
    i = pl.program_id(0)
    is_lat = i < lat_tiles
    per_seq = jnp.where(is_lat, lat_tiles_per_seq, ctx_tiles_per_seq)
    local = jnp.where(is_lat, i, i - lat_tiles) % per_seq
    first = local == 0
    last = local == per_seq - 1

    def glu(u):
        return u[:, :ch] * _sigmoid(u[:, ch:])

    z_ref[0:CONV_HALO] = jnp.where(first, 0.0, glu(prev_ref[...]))
    z_ref[CONV_HALO:CONV_HALO + tm] = glu(cur_ref[...])
    z_ref[CONV_HALO + tm:2 * CONV_HALO + tm] = jnp.where(last, 0.0, glu(next_ref[...]))
    half = (n_taps - 1) // 2
    rb = 128
    for cb in range(ch // 128):
        cols = slice(cb * 128, (cb + 1) * 128)
        for r in range(tm // rb):
            acc = jnp.zeros((rb, 128), f32)
            for k in range(n_taps):
                acc = acc + dww_ref[k:k + 1, cols] * z_ref[pl.ds(r * rb + CONV_HALO - half + k, rb), cols]
            y_ref[r * rb:(r + 1) * rb, cols] = acc + dwb_ref[:, cols]
    y = y_ref[...]
    mu = jnp.mean(y, axis=-1, keepdims=True)
    d = y - mu
    var = jnp.mean(d * d, axis=-1, keepdims=True)
    yn = d * lax.rsqrt(var + EPS) * lng_ref[...] + lnb_ref[...]
    act = (yn * _sigmoid(yn)).astype(bf16)
    o_ref[...] = (jnp.dot(act, pww_ref[...], preferred_element_type=f32) + pwb_ref[...]).astype(o_ref.dtype)


def _conformer(bu, dww, dwb, lng, lnb, pww, pwb, *, n_rows, n_lat, S, CTX, tm):
    T = bu.shape[0]
    ch = bu.shape[1] // 2
    n_taps = dww.shape[0]
    hb = tm // CONV_HALO
    n_hb = T // CONV_HALO
    kern = functools.partial(_conf_kernel, tm=tm, ch=ch, n_taps=n_taps, lat_tiles=n_lat // tm,
                             lat_tiles_per_seq=S // tm, ctx_tiles_per_seq=CTX // tm)
    row = lambda i: (0, 0)
    return pl.pallas_call(
        kern,
        out_shape=jax.ShapeDtypeStruct((n_rows, ch), bf16),
        grid=(n_rows // tm,),
        in_specs=[pl.BlockSpec((CONV_HALO, 2 * ch), lambda i: (jnp.maximum(i * hb - 1, 0), 0)),
                  pl.BlockSpec((tm, 2 * ch), lambda i: (i, 0)),
                  pl.BlockSpec((CONV_HALO, 2 * ch), lambda i: (jnp.minimum((i + 1) * hb, n_hb - 1), 0)),
                  pl.BlockSpec((n_taps, ch), row),
                  pl.BlockSpec((1, ch), row), pl.BlockSpec((1, ch), row), pl.BlockSpec((1, ch), row),
                  pl.BlockSpec((ch, ch), row), pl.BlockSpec((1, ch), row)],
        out_specs=pl.BlockSpec((tm, ch), lambda i: (i, 0)),
        scratch_shapes=[pltpu.VMEM((tm + 2 * CONV_HALO, ch), f32), pltpu.VMEM((tm, ch), f32)],
        compiler_params=_cp("arbitrary"),
        name="conformer",
    )(bu, bu, bu, dww, dwb.reshape(1, ch), lng.reshape(1, ch), lnb.reshape(1, ch), pww, pwb.reshape(1, ch))


def _mm_res_kernel(*refs, splits):
    n = len(splits)
    w_ref, x_ref, g_ref, o_ref = refs[n:]
    acc = None
    off = 0
    for r, k in zip(refs[:n], splits):
        part = jnp.dot(r[...], w_ref[off:off + k, :], preferred_element_type=f32)
        acc = part if acc is None else acc + part
        off += k
    o_ref[...] = x_ref[...] + g_ref[...] * acc


def _proj_residual(lhs, w, X, gate, *, n_rows, n_lat, S, B, tm, tn, name):
    K, N = w.shape
    splits = tuple(a.shape[1] for a in lhs)
    in_specs = [pl.BlockSpec((tm, k), lambda i, j: (i, 0)) for k in splits]
    in_specs += [pl.BlockSpec((K, tn), lambda i, j: (0, j)),
                 pl.BlockSpec((tm, tn), lambda i, j: (i, j)),
                 pl.BlockSpec((None, 1, tn), lambda i, j: (_mod_row(i, tm, n_lat, S, B), 0, j))]
    return pl.pallas_call(
        functools.partial(_mm_res_kernel, splits=splits),
        out_shape=jax.ShapeDtypeStruct((n_rows, N), f32),
        grid=(n_rows // tm, N // tn),
        in_specs=in_specs,
        out_specs=pl.BlockSpec((tm, tn), lambda i, j: (i, j)),
        compiler_params=_cp("arbitrary", "arbitrary"),
        name=name,
    )(*lhs, w, X, gate)


def _ffn_up_kernel(prev_ref, cur_ref, next_ref, wg_ref, wv_ref, cwg_ref, cwv_ref, cbg_ref, cbv_ref,
                   o_ref, hext_ref, *, tm, n_lat, S, CTX):
    i = pl.program_id(0)
    halo = BF16_ROW_TILE

    @pl.when(pl.program_id(1) == 0)
    def _():
        hext_ref[0:halo] = prev_ref[...]
        hext_ref[halo:halo + tm] = cur_ref[...]
        hext_ref[halo + tm:2 * halo + tm] = next_ref[...]

    tf = o_ref.shape[1]
    row = i * tm + lax.broadcasted_iota(jnp.int32, (tm, tf), 0)
    is_lat = row < n_lat
    pos = jnp.where(is_lat, row & (S - 1), (row - n_lat) & (CTX - 1))
    has_prev = pos != 0
    has_next = pos != jnp.where(is_lat, S - 1, CTX - 1)
    n_ext = tm + 2 * halo

    def branch(w_ref, cw_ref, cb_ref):
        u = jnp.dot(hext_ref[...], w_ref[...], preferred_element_type=f32)
        up = pltpu.roll(u, 1, 0)[halo:halo + tm]
        dn = pltpu.roll(u, n_ext - 1, 0)[halo:halo + tm]
        return (cw_ref[0:1, :] * jnp.where(has_prev, up, 0.0) + cw_ref[1:2, :] * u[halo:halo + tm]
                + cw_ref[2:3, :] * jnp.where(has_next, dn, 0.0) + cb_ref[...])

    gate = branch(wg_ref, cwg_ref, cbg_ref)
    val = branch(wv_ref, cwv_ref, cbv_ref)
    o_ref[...] = (gate * _sigmoid(gate) * val).astype(o_ref.dtype)


def _ffn_up(h, w_up, cw, cb, *, n_rows, n_lat, S, CTX, tm, tf):
    T, D = h.shape
    F = w_up.shape[1] // 2
    nf = F // tf
    halo = BF16_ROW_TILE
    hb = tm // halo
    n_hb = T // halo
    cb2 = cb.reshape(1, 2 * F)
    return pl.pallas_call(
        functools.partial(_ffn_up_kernel, tm=tm, n_lat=n_lat, S=S, CTX=CTX),
        out_shape=jax.ShapeDtypeStruct((n_rows, F), bf16),
        grid=(n_rows // tm, nf),
        in_specs=[pl.BlockSpec((halo, D), lambda i, j: (jnp.maximum(i * hb - 1, 0), 0)),
                  pl.BlockSpec((tm, D), lambda i, j: (i, 0)),
                  pl.BlockSpec((halo, D), lambda i, j: (jnp.minimum((i + 1) * hb, n_hb - 1), 0)),
                  pl.BlockSpec((D, tf), lambda i, j: (0, j)),
                  pl.BlockSpec((D, tf), lambda i, j: (0, nf + j)),
                  pl.BlockSpec((3, tf), lambda i, j: (0, j)),
                  pl.BlockSpec((3, tf), lambda i, j: (0, nf + j)),
                  pl.BlockSpec((1, tf), lambda i, j: (0, j)),
                  pl.BlockSpec((1, tf), lambda i, j: (0, nf + j))],
        out_specs=pl.BlockSpec((tm, tf), lambda i, j: (i, j)),
        scratch_shapes=[pltpu.VMEM((tm + 2 * halo, D), bf16)],
        compiler_params=_cp("arbitrary", "arbitrary"),
        name="ffn_up",
    )(h, h, h, w_up, w_up, cw, cw, cb2, cb2)


def _rope_tables(S, tm):
    t = jnp.arange(S, dtype=jnp.int32)
    row = (t // GRID_W).astype(f32)
    col = (t % GRID_W).astype(f32)
    half = HEAD_DIM // 2
    inv = ROPE_THETA ** (-jnp.arange(0, half, 2, dtype=f32) / half)
    ang = jnp.concatenate([row[:, None] * inv, col[:, None] * inv], axis=-1)
    cos = jnp.repeat(jnp.cos(ang), 2, axis=-1)
    sin = jnp.repeat(jnp.sin(ang), 2, axis=-1)
    sign = jnp.where(jnp.arange(HEAD_DIM) % 2 == 0, -1.0, 1.0).astype(f32)
    cos_t = jnp.concatenate([cos, jnp.ones((tm, HEAD_DIM), f32)], axis=0)
    sin_t = jnp.concatenate([sin * sign, jnp.zeros((tm, HEAD_DIM), f32)], axis=0)
    return cos_t, sin_t


def kernel(x, c, ctx, c_ctx, ada_w, ada_b, norm1_g, norm2_g, w_in, a_qn_g, a_kn_g, b_dw_w, b_dw_b,
           b_ln_g, b_ln_b, b_pw_w, b_pw_b, c_rpb, w_out, ffn_w_up, ffn_dw_w, ffn_dw_b, ffn_w_down, final_g):
    B, S, D = x.shape
    CTX = ctx.shape[1]
    L = ada_w.shape[0]
    n_slots = D // HEAD_DIM
    a_heads = 3 * n_slots // 8
    a_kv = a_heads // 3
    c_heads = 3 * n_slots // 8
    b_ch = D - (a_heads + c_heads) * HEAD_DIM
    a_q, a_kvw, c_w = a_heads * HEAD_DIM, a_kv * HEAD_DIM, c_heads * HEAD_DIM
    F = ffn_w_down.shape[1]
    n_lat, n_ctx = B * S, B * CTX
    T = n_lat + n_ctx
    rows = S // GRID_W
    assert B + 1 <= MOD_ROWS and rows % NA_ROWS == 0 and rows >= 2 * NA_ROWS
    assert S % CTX == 0 and n_lat % (NA_ROWS * GRID_W) == 0
    assert S & (S - 1) == 0 and CTX & (CTX - 1) == 0

    tm = _tile(n_ctx, 512, CTX) if n_ctx >= 512 else CTX
    tm_conv = _tile(CTX, 256, 128)
    tn = _tile(D, 512, 128)
    tf = _tile(F, 512, 128)
    tq = _tile(S, 512, 128)
    tq_ctx = _tile(CTX, 256, 128)

    X = jnp.concatenate([x.reshape(n_lat, D), ctx.reshape(n_ctx, D)], axis=0)
    cvec = jnp.concatenate([c, c_ctx[None, :], jnp.zeros((MOD_ROWS - B - 1, D), f32)], axis=0)
    mod = _ada_mod(cvec, ada_w, ada_b)
    cos_t, sin_t = _rope_tables(S, tm)
    w_in_b, w_out_b = w_in.astype(bf16), w_out.astype(bf16)
    w_up_b, w_down_b, pw_b = ffn_w_up.astype(bf16), ffn_w_down.astype(bf16), b_pw_w.astype(bf16)
    qk_w = a_q + a_kvw
    col_v, col_bu, col_n = qk_w, qk_w + a_kvw, qk_w + a_kvw + 2 * b_ch

    for l in range(L):
        ctx_needed = l < L - 1
        n_rows = T if ctx_needed else n_lat
        m = mod[l].reshape(MOD_ROWS, N_MOD, 1, D)
        sh1, sc1, g1, sh2, sc2, g2 = (m[:, k] for k in range(N_MOD))
        common = dict(n_lat=n_lat, S=S, B=B, tm=tm)

        h = _norm_mod(X, norm1_g[l], sh1, sc1, **common)
        gains = jnp.concatenate([jnp.tile(a_qn_g[l], a_heads), jnp.tile(a_kn_g[l], a_kv)]).reshape(1, qk_w)
        qk = _proj_qk(h, w_in_b[l], gains, cos_t, sin_t, qk_w, tm=tm, n_lat=n_lat, S=S)
        va = _proj(h, w_in_b[l], col_v, a_kvw, bf16, tm=tm, tn_pref=512, name="proj_v")
        bu = _proj(h, w_in_b[l], col_bu, 2 * b_ch, f32, tm=tm, tn_pref=1024, name="proj_bu")
        nqkv = _proj(h, w_in_b[l], col_n, 3 * c_w, bf16, tm=tm, tn_pref=512, name="proj_n")

        seg_ctx = (qk, a_q, va, 0, n_lat, CTX)
        seg_lat = (qk, a_q, va, 0, 0, S)
        o_a = _attention(qk, 0, [seg_ctx, seg_lat], a_q, nb=B, heads=a_heads, group=a_heads // a_kv,
                         q_row0=0, q_len=S, tq=tq, out_rows=n_rows, name="gqa_latent")
        bias = _na_bias(c_rpb[l], rows=rows)
        o_c = _na_attention(nqkv, bias, B=B, S=S, CTX=CTX, heads=c_heads, out_rows=n_rows)
        if ctx_needed:
            o_a = _attention(qk, 0, [seg_ctx], a_q, nb=B, heads=a_heads, group=a_heads // a_kv,
                             q_row0=n_lat, q_len=CTX, tq=tq_ctx, prev_out=o_a, name="gqa_context")
            nseg = (nqkv, c_w, nqkv, 2 * c_w, n_lat, CTX)
            o_c = _attention(nqkv, 0, [nseg], c_w, nb=B, heads=c_heads, group=1,
                             q_row0=n_lat, q_len=CTX, tq=tq_ctx, prev_out=o_c, name="mha_context")
        o_b = _conformer(bu, b_dw_w[l], b_dw_b[l], b_ln_g[l], b_ln_b[l], pw_b[l], b_pw_b[l],
                         n_rows=n_rows, n_lat=n_lat, S=S, CTX=CTX, tm=tm_conv)
        X = _proj_residual([o_a, o_b, o_c], w_out_b[l], X, g1, n_rows=n_rows, tn=tn, name="proj_out", **common)

        h2 = _norm_mod(X, norm2_g[l], sh2, sc2, **common)
        act = _ffn_up(h2, w_up_b[l], ffn_dw_w[l], ffn_dw_b[l], n_rows=n_rows, n_lat=n_lat, S=S, CTX=CTX,
                      tm=tm, tf=tf)
        X = _proj_residual([act], w_down_b[l], X, g2, n_rows=n_rows, tn=tn, name="ffn_down", **common)

    out = _final_norm(X, final_g, rows=n_lat, tm=_tile(n_lat, 256, 8))
    return out.reshape(B, S, D)
```

```python
import functools
import math

import jax
import jax.numpy as jnp
from jax import lax
from jax.experimental import pallas as pl
from jax.experimental.pallas import tpu as pltpu

HEAD_DIM = 128
GRID_W = 64
GRID_SHIFT = 6
NA_ROWS = 8
NA_COLS = 16
ROPE_THETA = 10000.0
EPS = 1e-6
N_MOD = 6
NEG_INF = -1e30
LANES = 128
SUBLANES = 8

V7X_VMEM_LIMIT_BYTES = 56 * 1024 * 1024
MOD_ROWS = 8
HALO = 16
ROW_CHUNKS = 2

f32 = jnp.float32
bf16 = jnp.bfloat16
_NT = (((1,), (1,)), ((), ()))


def _cp(*sem):
    return pltpu.CompilerParams(dimension_semantics=sem, vmem_limit_bytes=V7X_VMEM_LIMIT_BYTES)


def _tile(n, pref, unit):
    t = (min(pref, n) // unit) * unit
    while n % t:
        t -= unit
    return t


def _sigmoid(x):
    return 1.0 / (1.0 + jnp.exp(-x))


def _mod_row(i, tm, n_lat, S, B):
    return jnp.where(i * tm < n_lat, (i * tm) // S, B)


def _cast_weights_once(w_ref, wb_ref):
    @pl.when(pl.program_id(1) == 0)
    def _():
        wb_ref[...] = w_ref[...].astype(bf16)


def _mod_kernel(c_ref, w_ref, b_ref, o_ref):
    c = c_ref[...]
    sc = (c * _sigmoid(c)).astype(bf16)
    o_ref[...] = jnp.dot(sc, w_ref[...].astype(bf16), preferred_element_type=f32) + b_ref[...]


def _ada_mod(cvec, ada_w, ada_b):
    L, D, N = ada_w.shape
    tn = _tile(N, 512, LANES)
    return pl.pallas_call(
        _mod_kernel,
        out_shape=jax.ShapeDtypeStruct((L, MOD_ROWS, N), f32),
        grid=(L, N // tn),
        in_specs=[pl.BlockSpec((MOD_ROWS, D), lambda l, j: (0, 0)),
                  pl.BlockSpec((None, D, tn), lambda l, j: (l, 0, j)),
                  pl.BlockSpec((None, 1, tn), lambda l, j: (l, 0, j))],
        out_specs=pl.BlockSpec((None, MOD_ROWS, tn), lambda l, j: (l, 0, j)),
        compiler_params=_cp("arbitrary", "arbitrary"),
        name="ada_mod",
    )(cvec, ada_w, ada_b.reshape(L, 1, N))


def _rms_mod(x, g, sh, sc):
    y = x * lax.rsqrt(jnp.mean(x * x, axis=-1, keepdims=True) + EPS) * g
    return y * (1.0 + sc) + sh


def _norm_mod_kernel(x_ref, g_ref, sh_ref, sc_ref, o_ref):
    o_ref[...] = _rms_mod(x_ref[...], g_ref[...], sh_ref[...], sc_ref[...]).astype(o_ref.dtype)


def _norm_mod_halo_kernel(prev_ref, x_ref, next_ref, g_ref, sh_ref, sc_ref, o_ref, *, per):
    s = pl.program_id(1)
    sub = x_ref.shape[0]
    g, sh, sc = g_ref[...], sh_ref[...], sc_ref[...]
    o_ref[pl.ds(pl.multiple_of(HALO + s * sub, HALO), sub)] = _rms_mod(x_ref[...], g, sh, sc).astype(o_ref.dtype)

    @pl.when(s == 0)
    def _():
        o_ref[0:HALO] = _rms_mod(prev_ref[...], g, sh, sc).astype(o_ref.dtype)

    @pl.when(s == per - 1)
    def _():
        o_ref[HALO + per * sub:2 * HALO + per * sub] = _rms_mod(next_ref[...], g, sh, sc).astype(o_ref.dtype)


def _norm_kernel(x_ref, g_ref, o_ref):
    x = x_ref[...]
    o_ref[...] = x * lax.rsqrt(jnp.mean(x * x, axis=-1, keepdims=True) + EPS) * g_ref[...]


def _norm_mod(X, g, sh, sc, *, n_lat, S, B, tm):
    T, D = X.shape
    mrow = lambda i: (_mod_row(i, tm, n_lat, S, B), 0, 0)
    return pl.pallas_call(
        _norm_mod_kernel,
        out_shape=jax.ShapeDtypeStruct((T, D), bf16),
        grid=(T // tm,),
        in_specs=[pl.BlockSpec((tm, D), lambda i: (i, 0)),
                  pl.BlockSpec((1, D), lambda i: (0, 0)),
                  pl.BlockSpec((None, 1, D), mrow),
                  pl.BlockSpec((None, 1, D), mrow)],
        out_specs=pl.BlockSpec((tm, D), lambda i: (i, 0)),
        compiler_params=_cp("arbitrary"),
        name="norm_mod",
    )(X, g.reshape(1, D), sh, sc)


def _norm_mod_halo(X, g, sh, sc, *, n_lat, S, B, tm, sub):
    T, D = X.shape
    assert tm % sub == 0
    hb = tm // HALO
    n_hb = T // HALO
    per = tm // sub
    mrow = lambda i, s: (_mod_row(i, tm, n_lat, S, B), 0, 0)
    return pl.pallas_call(
        functools.partial(_norm_mod_halo_kernel, per=per),
        out_shape=jax.ShapeDtypeStruct((T // tm, tm + 2 * HALO, D), bf16),
        grid=(T // tm, per),
        in_specs=[pl.BlockSpec((HALO, D), lambda i, s: (jnp.maximum(i * hb - 1, 0), 0)),
                  pl.BlockSpec((sub, D), lambda i, s: (i * per + s, 0)),
                  pl.BlockSpec((HALO, D), lambda i, s: (jnp.minimum((i + 1) * hb, n_hb - 1), 0)),
                  pl.BlockSpec((1, D), lambda i, s: (0, 0)),
                  pl.BlockSpec((None, 1, D), mrow),
                  pl.BlockSpec((None, 1, D), mrow)],
        out_specs=pl.BlockSpec((None, tm + 2 * HALO, D), lambda i, s: (i, 0, 0)),
        compiler_params=_cp("arbitrary", "arbitrary"),
        name="norm_mod_halo",
    )(X, X, X, g.reshape(1, D), sh, sc)


def _final_norm(X, g, *, rows, tm):
    D = X.shape[1]
    return pl.pallas_call(
        _norm_kernel,
        out_shape=jax.ShapeDtypeStruct((rows, D), f32),
        grid=(rows // tm,),
        in_specs=[pl.BlockSpec((tm, D), lambda i: (i, 0)),
                  pl.BlockSpec((1, D), lambda i: (0, 0))],
        out_specs=pl.BlockSpec((tm, D), lambda i: (i, 0)),
        compiler_params=_cp("arbitrary"),
        name="final_norm",
    )(X, g.reshape(1, D))


def _mm_kernel(h_ref, w_ref, o_ref, wb_ref):
    _cast_weights_once(w_ref, wb_ref)
    o_ref[...] = jnp.dot(h_ref[...], wb_ref[...], preferred_element_type=f32).astype(o_ref.dtype)


def _proj(h, w, l, col0, ncols, out_dtype, *, tm, tn_pref, name):
    T, D = h.shape
    tn = _tile(ncols, tn_pref, LANES)
    while col0 % tn or ncols % tn:
        tn -= LANES
    c0 = col0 // tn
    return pl.pallas_call(
        _mm_kernel,
        out_shape=jax.ShapeDtypeStruct((T, ncols), out_dtype),
        grid=(ncols // tn, T // tm),
        in_specs=[pl.BlockSpec((tm, D), lambda j, i: (i, 0)),
                  pl.BlockSpec((None, D, tn), lambda j, i: (l, 0, c0 + j))],
        out_specs=pl.BlockSpec((tm, tn), lambda j, i: (i, j)),
        scratch_shapes=[pltpu.VMEM((D, tn), bf16)],
        compiler_params=_cp("arbitrary", "arbitrary"),
        name=name,
    )(h, w)


def _mm_qk_kernel(h_ref, w_ref, g_ref, cos_ref, sin_ref, o_ref, wb_ref):
    _cast_weights_once(w_ref, wb_ref)
    tm, tn = o_ref.shape
    rc = tm // ROW_CHUNKS
    even = (lax.broadcasted_iota(jnp.int32, (rc, HEAD_DIM), 1) & 1) == 0
    for r in range(ROW_CHUNKS):
        rows = slice(r * rc, (r + 1) * rc)
        acc = jnp.dot(h_ref[rows, :], wb_ref[...], preferred_element_type=f32)
        cosv = cos_ref[rows, :]
        sinv = sin_ref[rows, :]
        for hh in range(tn // HEAD_DIM):
            sl = slice(hh * HEAD_DIM, (hh + 1) * HEAD_DIM)
            blk = acc[:, sl]
            y = blk * lax.rsqrt(jnp.mean(blk * blk, axis=-1, keepdims=True) + EPS) * g_ref[:, sl]
            partner = jnp.where(even, pltpu.roll(y, HEAD_DIM - 1, 1), pltpu.roll(y, 1, 1))
            o_ref[rows, sl] = (y * cosv + partner * sinv).astype(o_ref.dtype)


def _proj_qk(h, w, l, gains, cos_t, sin_t, ncols, *, tm, n_lat, S):
    T, D = h.shape
    tn = _tile(ncols, 512, LANES)
    pos_blocks = S // tm

    def pos(j, i):
        return (jnp.where(i * tm < n_lat, i % pos_blocks, pos_blocks), 0)

    return pl.pallas_call(
        _mm_qk_kernel,
        out_shape=jax.ShapeDtypeStruct((T, ncols), bf16),
        grid=(ncols // tn, T // tm),
        in_specs=[pl.BlockSpec((tm, D), lambda j, i: (i, 0)),
                  pl.BlockSpec((None, D, tn), lambda j, i: (l, 0, j)),
                  pl.BlockSpec((1, tn), lambda j, i: (0, j)),
                  pl.BlockSpec((tm, HEAD_DIM), pos),
                  pl.BlockSpec((tm, HEAD_DIM), pos)],
        out_specs=pl.BlockSpec((tm, tn), lambda j, i: (i, j)),
        scratch_shapes=[pltpu.VMEM((D, tn), bf16)],
        compiler_params=_cp("arbitrary", "arbitrary"),
        name="proj_qk",
    )(h, w, gains, cos_t, sin_t)


def _lane_max(s):
    mx = s[:, :LANES]
    for t in range(1, s.shape[1] // LANES):
        mx = jnp.maximum(mx, s[:, t * LANES:(t + 1) * LANES])
    return mx


def _attn_kernel(*refs, n_seg, group, scale):
    q_ref = refs[0]
    kv = refs[1:1 + 2 * n_seg]
    o_ref = refs[1 + 2 * n_seg]
    for g in range(group):
        sl = slice(g * HEAD_DIM, (g + 1) * HEAD_DIM)
        q = q_ref[:, sl]
        ss = [lax.dot_general(q, kv[2 * i][...], _NT, preferred_element_type=f32) * scale
              for i in range(n_seg)]
        m = jnp.max(ss[0], axis=-1, keepdims=True)
        for s in ss[1:]:
            m = jnp.maximum(m, jnp.max(s, axis=-1, keepdims=True))
        l = None
        o = None
        for i, s in enumerate(ss):
            p = jnp.exp(s - m)
            li = jnp.sum(p, axis=-1, keepdims=True)
            oi = jnp.dot(p.astype(bf16), kv[2 * i + 1][...], preferred_element_type=f32)
            l = li if l is None else l + li
            o = oi if o is None else o + oi
        o_ref[:, sl] = (o / l).astype(o_ref.dtype)


def _attn_kernel_aliased(*refs, n_seg, group, scale):
    n_in = 1 + 2 * n_seg
    _attn_kernel(*refs[:n_in], refs[n_in + 1], n_seg=n_seg, group=group, scale=scale)


def _context_attention(q_arr, kv_seg, prev_out, *, nb, heads, group, q_row0, q_len, tq, name):
    gw = group * HEAD_DIM
    n_kv = heads // group
    nq = q_len // tq
    qb0 = q_row0 // tq
    k_arr, k_col0, v_arr, v_col0, row0, length = kv_seg
    rb0, kc0, vc0 = row0 // length, k_col0 // HEAD_DIM, v_col0 // HEAD_DIM
    return pl.pallas_call(
        functools.partial(_attn_kernel_aliased, n_seg=1, group=group, scale=HEAD_DIM ** -0.5),
        out_shape=jax.ShapeDtypeStruct(prev_out.shape, bf16),
        grid=(n_kv, nb, nq),
        in_specs=[pl.BlockSpec((tq, gw), lambda j, b, i: (qb0 + b * nq + i, j)),
                  pl.BlockSpec((length, HEAD_DIM), lambda j, b, i: (rb0 + b, kc0 + j)),
                  pl.BlockSpec((length, HEAD_DIM), lambda j, b, i: (rb0 + b, vc0 + j)),
                  pl.BlockSpec(memory_space=pl.ANY)],
        out_specs=pl.BlockSpec((tq, gw), lambda j, b, i: (qb0 + b * nq + i, j)),
        input_output_aliases={3: 0},
        compiler_params=_cp("arbitrary", "arbitrary", "arbitrary"),
        name=name,
    )(q_arr, k_arr, v_arr, prev_out)


def _gqa_kernel(q_ref, kc_ref, vc_ref, kl_ref, vl_ref, o_ref,
                qs_ref, s_ref, sc_ref, m_ref, l_ref, acc_ref, *, group, kchunk, scale, unroll):
    tq = q_ref.shape[0]
    for g in range(group):
        qs_ref[g * tq:(g + 1) * tq, :] = q_ref[:, g * HEAD_DIM:(g + 1) * HEAD_DIM]
    q = qs_ref[...]
    n_chunks = kl_ref.shape[0] // kchunk
    c1 = scale * math.log2(math.e)

    s0 = lax.dot_general(q, kc_ref[...], _NT, preferred_element_type=f32) * c1
    sc_ref[...] = s0
    m_ref[...] = _lane_max(s0)

    def pass1(c, carry):
        k = kl_ref[pl.ds(pl.multiple_of(c * kchunk, kchunk), kchunk), :]
        s = lax.dot_general(q, k, _NT, preferred_element_type=f32) * c1
        s_ref[c] = s
        m_ref[...] = jnp.maximum(m_ref[...], _lane_max(s))
        return carry

    lax.fori_loop(0, n_chunks, pass1, 0, unroll=unroll)
    m_ref[...] = jnp.broadcast_to(jnp.max(m_ref[...], axis=-1, keepdims=True), m_ref.shape)

    def probs(s):
        mb = m_ref[...]
        ps = [jnp.exp2(s[:, t * LANES:(t + 1) * LANES] - mb) for t in range(s.shape[1] // LANES)]
        tot = ps[0]
        for p in ps[1:]:
            tot = tot + p
        return jnp.concatenate(ps, axis=1).astype(bf16), tot

    p0, t0 = probs(sc_ref[...])
    l_ref[...] = t0
    acc_ref[...] = jnp.dot(p0, vc_ref[...], preferred_element_type=f32)

    def pass2(c, carry):
        p, tot = probs(s_ref[c])
        l_ref[...] += tot
        v = vl_ref[pl.ds(pl.multiple_of(c * kchunk, kchunk), kchunk), :]
        acc_ref[...] += jnp.dot(p, v, preferred_element_type=f32)
        return carry

    lax.fori_loop(0, n_chunks, pass2, 0, unroll=unroll)
    o = acc_ref[...] / jnp.sum(l_ref[...], axis=-1, keepdims=True)
    for g in range(group):
        o_ref[:, g * HEAD_DIM:(g + 1) * HEAD_DIM] = o[g * tq:(g + 1) * tq].astype(o_ref.dtype)


def _gqa_latent(qk, va, *, B, S, CTX, heads, group, q_cols, out_rows, tq, kchunk):
    gw = group * HEAD_DIM
    n_kv = heads // group
    nq = S // tq
    kc0 = q_cols // HEAD_DIM
    ctx_b0 = (B * S) // CTX
    M = group * tq
    return pl.pallas_call(
        functools.partial(_gqa_kernel, group=group, kchunk=kchunk, scale=HEAD_DIM ** -0.5, unroll=4),
        out_shape=jax.ShapeDtypeStruct((out_rows, q_cols), bf16),
        grid=(n_kv, B, nq),
        in_specs=[pl.BlockSpec((tq, gw), lambda j, b, i: (b * nq + i, j)),
                  pl.BlockSpec((CTX, HEAD_DIM), lambda j, b, i: (ctx_b0 + b, kc0 + j)),
                  pl.BlockSpec((CTX, HEAD_DIM), lambda j, b, i: (ctx_b0 + b, j)),
                  pl.BlockSpec((S, HEAD_DIM), lambda j, b, i: (b, kc0 + j)),
                  pl.BlockSpec((S, HEAD_DIM), lambda j, b, i: (b, j))],
        out_specs=pl.BlockSpec((tq, gw), lambda j, b, i: (b * nq + i, j)),
        scratch_shapes=[pltpu.VMEM((M, HEAD_DIM), bf16),
                        pltpu.VMEM((S // kchunk, M, kchunk), f32),
                        pltpu.VMEM((M, CTX), f32),
                        pltpu.VMEM((M, LANES), f32),
                        pltpu.VMEM((M, LANES), f32),
                        pltpu.VMEM((M, HEAD_DIM), f32)],
        compiler_params=_cp("arbitrary", "arbitrary", "arbitrary"),
        name="gqa_latent",
    )(qk, qk, va, qk, va)


def _na_bias_kernel(rpb_ref, o_ref, tw_ref, *, rows):
    h = pl.program_id(0)
    case = pl.program_id(1)
    n_r = 2 * NA_ROWS - 1
    n_c = 2 * NA_COLS - 1
    per_head = n_r * n_c
    shape = (GRID_W, 2 * GRID_W)
    qc = lax.broadcasted_iota(jnp.int32, shape, 0)
    lane = lax.broadcasted_iota(jnp.int32, shape, 1)
    kc = lane & (GRID_W - 1)
    second = lane >= GRID_W
    dc = kc - qc
    cs = jnp.clip(qc - NA_COLS // 2, 0, GRID_W - NA_COLS)
    col_ok = (kc >= cs) & (kc < cs + NA_COLS)
    neg = jnp.full(shape, NEG_INF, f32)
    lo, hi = -NA_ROWS, NA_ROWS - 1
    span = 2 * NA_ROWS - 1
    for a in range(-span, span):
        if lo <= a <= hi:
            acc = neg
            for bb in range(n_c):
                s0 = rpb_ref[h * per_head + (a + NA_ROWS - 1) * n_c + bb] if -NA_ROWS < a < NA_ROWS else NEG_INF
                s1 = rpb_ref[h * per_head + (a + NA_ROWS) * n_c + bb] if -NA_ROWS < a + 1 < NA_ROWS else NEG_INF
                acc = jnp.where(dc == bb - (NA_COLS - 1), jnp.where(second, s1, s0), acc)
            tw_ref[a + span] = jnp.where(col_ok, acc, NEG_INF)
        else:
            tw_ref[a + span] = neg
    qbase = jnp.where(case == 0, 0, jnp.where(case == 1, NA_ROWS, rows - NA_ROWS))
    kbase = jnp.where(case == 0, 0, jnp.where(case == 1, NA_ROWS // 2, rows - 2 * NA_ROWS))
    off = kbase - qbase
    pair_w = 2 * GRID_W
    chunk_w = 2 * pair_w
    for qr in range(NA_ROWS):
        for kp in range(NA_ROWS):
            a = 2 * kp - qr + off
            lane0 = (kp % 2) * pair_w
            o_ref[kp // 2, qr * GRID_W:(qr + 1) * GRID_W, lane0:lane0 + pair_w] = tw_ref[a + span]
    full = (NA_ROWS * GRID_W, chunk_w)
    qrow = qbase + (lax.broadcasted_iota(jnp.int32, full, 0) >> GRID_SHIFT)
    rs = jnp.clip(qrow - NA_ROWS // 2, 0, rows - NA_ROWS)
    for ch in range(4):
        krow = kbase + 4 * ch + (lax.broadcasted_iota(jnp.int32, full, 1) >> GRID_SHIFT)
        valid = (krow >= rs) & (krow < rs + NA_ROWS)
        o_ref[ch] = jnp.where(valid, o_ref[ch], NEG_INF)


def _na_bias(rpb_l, *, rows):
    H = rpb_l.shape[0]
    tq, kb = NA_ROWS * GRID_W, 4 * GRID_W
    return pl.pallas_call(
        functools.partial(_na_bias_kernel, rows=rows),
        out_shape=jax.ShapeDtypeStruct((H, 3, 4, tq, kb), f32),
        grid=(H, 3),
        in_specs=[pl.BlockSpec(memory_space=pltpu.SMEM)],
        out_specs=pl.BlockSpec((None, None, 4, tq, kb), lambda h, c: (h, c, 0, 0, 0)),
        scratch_shapes=[pltpu.VMEM((2 * (2 * NA_ROWS - 1), GRID_W, 2 * GRID_W), f32)],
        compiler_params=_cp("arbitrary", "arbitrary"),
        name="na_bias",
    )(rpb_l.reshape(-1))


def _na_kernel(q_ref, k0, k1, k2, k3, v0, v1, v2, v3, kc_ref, vc_ref, bias_ref, o_ref,
               s_ref, sc_ref, m_ref, l_ref, acc_ref, *, scale, hp):
    ks = (k0, k1, k2, k3)
    vs = (v0, v1, v2, v3)
    for hh in range(hp):
        sl = slice(hh * HEAD_DIM, (hh + 1) * HEAD_DIM)
        q = q_ref[:, sl]
        s0 = lax.dot_general(q, kc_ref[:, sl], _NT, preferred_element_type=f32) * scale
        sc_ref[hh] = s0
        m_ref[hh] = _lane_max(s0)
        for i in range(4):
            s = lax.dot_general(q, ks[i][:, sl], _NT, preferred_element_type=f32) * scale + bias_ref[hh, i]
            s_ref[hh, i] = s
            m_ref[hh] = jnp.maximum(m_ref[hh], _lane_max(s))
        m_ref[hh] = jnp.broadcast_to(jnp.max(m_ref[hh], axis=-1, keepdims=True), m_ref.shape[1:])

        def probs(s):
            mb = m_ref[hh]
            ps = [jnp.exp(s[:, t * LANES:(t + 1) * LANES] - mb) for t in range(s.shape[1] // LANES)]
            tot = ps[0]
            for p in ps[1:]:
                tot = tot + p
            return jnp.concatenate(ps, axis=1).astype(bf16), tot

        p0, t0 = probs(sc_ref[hh])
        l_ref[hh] = t0
        acc_ref[hh] = jnp.dot(p0, vc_ref[:, sl], preferred_element_type=f32)
        for i in range(4):
            p, tot = probs(s_ref[hh, i])
            l_ref[hh] += tot
            acc_ref[hh] += jnp.dot(p, vs[i][:, sl], preferred_element_type=f32)
        o_ref[:, sl] = (acc_ref[hh] / jnp.sum(l_ref[hh], axis=-1, keepdims=True)).astype(o_ref.dtype)


def _na_attention(nqkv, bias, *, B, S, CTX, heads, out_rows):
    rows = S // GRID_W
    tq = NA_ROWS * GRID_W
    kb = tq // 2
    nt = rows // NA_ROWS
    kb_per_seq = S // kb
    ctx_b0 = (B * S) // CTX

    def case(t):
        return jnp.where(t == 0, 0, jnp.where(t == nt - 1, 2, 1))

    def kblock(t):
        return jnp.clip(2 * t - 1, 0, kb_per_seq - 4)

    hp = 4 if heads % 4 == 0 else (2 if heads % 2 == 0 else 1)
    hw = hp * HEAD_DIM
    hblocks = heads // hp

    def kspec(col0, jj):
        return pl.BlockSpec((kb, hw), lambda h, t, b: (b * kb_per_seq + kblock(t) + jj, col0 + h))

    in_specs = [pl.BlockSpec((tq, hw), lambda h, t, b: (b * nt + t, h))]
    in_specs += [kspec(hblocks, jj) for jj in range(4)]
    in_specs += [kspec(2 * hblocks, jj) for jj in range(4)]
    in_specs += [pl.BlockSpec((CTX, hw), lambda h, t, b: (ctx_b0 + b, hblocks + h)),
                 pl.BlockSpec((CTX, hw), lambda h, t, b: (ctx_b0 + b, 2 * hblocks + h)),
                 pl.BlockSpec((hp, None, 4, tq, kb), lambda h, t, b: (h, case(t), 0, 0, 0))]
    return pl.pallas_call(
        functools.partial(_na_kernel, scale=HEAD_DIM ** -0.5, hp=hp),
        out_shape=jax.ShapeDtypeStruct((out_rows, heads * HEAD_DIM), bf16),
        grid=(hblocks, nt, B),
        in_specs=in_specs,
        out_specs=pl.BlockSpec((tq, hw), lambda h, t, b: (b * nt + t, h)),
        scratch_shapes=[pltpu.VMEM((hp, 4, tq, kb), f32),
                        pltpu.VMEM((hp, tq, CTX), f32),
                        pltpu.VMEM((hp, tq, LANES), f32),
                        pltpu.VMEM((hp, tq, LANES), f32),
                        pltpu.VMEM((hp, tq, HEAD_DIM), f32)],
        compiler_params=_cp("arbitrary", "arbitrary", "arbitrary"),
        name="na_attn",
    )(*([nqkv] * 11), bias)


def _conf_kernel(prev_ref, cur_ref, next_ref, dww_ref, dwb_ref, lng_ref, lnb_ref, pww_ref, pwb_ref,
                 o_ref, z_ref, zs_ref, y_ref, pwb16_ref, *, tm, ch, n_taps, lat_tiles, lat_tiles_per_seq,
                 ctx_tiles_per_seq):
    i = pl.program_id(0)

    @pl.when(i == 0)
    def _():
        pwb16_ref[...] = pww_ref[...].astype(bf16)

    is_lat = i < lat_tiles
    per_seq = jnp.where(is_lat, lat_tiles_per_seq, ctx_tiles_per_seq)
    local = jnp.where(is_lat, i, i - lat_tiles) % per_seq
    first = local == 0
    last = local == per_seq - 1

    def glu(u):
        return u[:, :ch] * _sigmoid(u[:, ch:])

    z_ref[0:HALO] = jnp.where(first, 0.0, glu(prev_ref[...]))
    z_ref[HALO:HALO + tm] = glu(cur_ref[...])
    z_ref[HALO + tm:2 * HALO + tm] = jnp.where(last, 0.0, glu(next_ref[...]))
    half = (n_taps - 1) // 2
    n_sh = tm + 2 * HALO - SUBLANES
    for b in range(1, SUBLANES):
        zs_ref[b - 1, 0:n_sh, :] = z_ref[b:b + n_sh, :]
    rb = 128
    for cb in range(ch // LANES):
        cols = slice(cb * LANES, (cb + 1) * LANES)
        for r in range(tm // rb):
            acc = jnp.zeros((rb, LANES), f32)
            for k in range(n_taps):
                a, b = divmod(HALO - half + k, SUBLANES)
                r0 = r * rb + a * SUBLANES
                win = z_ref[r0:r0 + rb, cols] if b == 0 else zs_ref[b - 1, r0:r0 + rb, cols]
                acc = acc + dww_ref[k:k + 1, cols] * win
            y_ref[r * rb:(r + 1) * rb, cols] = acc + dwb_ref[:, cols]
    y = y_ref[...]
    mu = jnp.mean(y, axis=-1, keepdims=True)
    d = y - mu
    var = jnp.mean(d * d, axis=-1, keepdims=True)
    yn = d * lax.rsqrt(var + EPS) * lng_ref[...] + lnb_ref[...]
    act = (yn * _sigmoid(yn)).astype(bf16)
    o_ref[...] = (jnp.dot(act, pwb16_ref[...], preferred_element_type=f32) + pwb_ref[...]).astype(o_ref.dtype)


def _conformer(bu, dww, dwb, lng, lnb, pww, pwb, *, n_rows, n_lat, S, CTX, tm):
    T = bu.shape[0]
    ch = bu.shape[1] // 2
    n_taps = dww.shape[0]
    hb = tm // HALO
    n_hb = T // HALO
    kern = functools.partial(_conf_kernel, tm=tm, ch=ch, n_taps=n_taps, lat_tiles=n_lat // tm,
                             lat_tiles_per_seq=S // tm, ctx_tiles_per_seq=CTX // tm)
    row = lambda i: (0, 0)
    return pl.pallas_call(
        kern,
        out_shape=jax.ShapeDtypeStruct((n_rows, ch), bf16),
        grid=(n_rows // tm,),
        in_specs=[pl.BlockSpec((HALO, 2 * ch), lambda i: (jnp.maximum(i * hb - 1, 0), 0)),
                  pl.BlockSpec((tm, 2 * ch), lambda i: (i, 0)),
                  pl.BlockSpec((HALO, 2 * ch), lambda i: (jnp.minimum((i + 1) * hb, n_hb - 1), 0)),
                  pl.BlockSpec((n_taps, ch), row),
                  pl.BlockSpec((1, ch), row), pl.BlockSpec((1, ch), row), pl.BlockSpec((1, ch), row),
                  pl.BlockSpec((ch, ch), row), pl.BlockSpec((1, ch), row)],
        out_specs=pl.BlockSpec((tm, ch), lambda i: (i, 0)),
        scratch_shapes=[pltpu.VMEM((tm + 2 * HALO, ch), f32),
                        pltpu.VMEM((SUBLANES - 1, tm + 2 * HALO, ch), f32), pltpu.VMEM((tm, ch), f32),
                        pltpu.VMEM((ch, ch), bf16)],
        compiler_params=_cp("arbitrary"),
        name="conformer",
    )(bu, bu, bu, dww, dwb.reshape(1, ch), lng.reshape(1, ch), lnb.reshape(1, ch), pww, pwb.reshape(1, ch))


def _mm_res_kernel(*refs, splits):
    n = len(splits)
    w_ref, x_ref, g_ref, o_ref, wb_ref = refs[n:]
    _cast_weights_once(w_ref, wb_ref)
    acc = None
    off = 0
    for r, k in zip(refs[:n], splits):
        part = jnp.dot(r[...], wb_ref[off:off + k, :], preferred_element_type=f32)
        acc = part if acc is None else acc + part
        off += k
    o_ref[...] = x_ref[...] + g_ref[...] * acc


def _proj_residual(lhs, w, l, X, gate, *, n_rows, n_lat, S, B, tm, tn, name):
    _, K, N = w.shape
    splits = tuple(a.shape[1] for a in lhs)
    in_specs = [pl.BlockSpec((tm, k), lambda j, i: (i, 0)) for k in splits]
    in_specs += [pl.BlockSpec((None, K, tn), lambda j, i: (l, 0, j)),
                 pl.BlockSpec((tm, tn), lambda j, i: (i, j)),
                 pl.BlockSpec((None, 1, tn), lambda j, i: (_mod_row(i, tm, n_lat, S, B), 0, j))]
    return pl.pallas_call(
        functools.partial(_mm_res_kernel, splits=splits),
        out_shape=jax.ShapeDtypeStruct((n_rows, N), f32),
        grid=(N // tn, n_rows // tm),
        in_specs=in_specs,
        out_specs=pl.BlockSpec((tm, tn), lambda j, i: (i, j)),
        scratch_shapes=[pltpu.VMEM((K, tn), bf16)],
        compiler_params=_cp("arbitrary", "arbitrary"),
        name=name,
    )(*lhs, w, X, gate)


def _ffn_up_kernel(h_ref, wg_ref, wv_ref, cwg_ref, cwv_ref, cbg_ref, cbv_ref, mp_ref, mn_ref, o_ref,
                   wgb_ref, wvb_ref, *, tm):
    _cast_weights_once(wg_ref, wgb_ref)
    _cast_weights_once(wv_ref, wvb_ref)
    tf = o_ref.shape[1]
    rc = tm // ROW_CHUNKS
    n_ext = rc + 2 * HALO
    for r in range(ROW_CHUNKS):
        rows = slice(r * rc, (r + 1) * rc)
        h = h_ref[r * rc:r * rc + n_ext, :]
        has_prev = jnp.concatenate([mp_ref[rows, :]] * (tf // LANES), axis=1)
        has_next = jnp.concatenate([mn_ref[rows, :]] * (tf // LANES), axis=1)

        def branch(wb_ref, cw_ref, cb_ref):
            u = jnp.dot(h, wb_ref[...], preferred_element_type=f32)
            up = pltpu.roll(u, 1, 0)[HALO:HALO + rc]
            dn = pltpu.roll(u, n_ext - 1, 0)[HALO:HALO + rc]
            return (cw_ref[0:1, :] * (up * has_prev) + cw_ref[1:2, :] * u[HALO:HALO + rc]
                    + cw_ref[2:3, :] * (dn * has_next) + cb_ref[...])

        gate = branch(wgb_ref, cwg_ref, cbg_ref)
        val = branch(wvb_ref, cwv_ref, cbv_ref)
        o_ref[rows, :] = (gate * _sigmoid(gate) * val).astype(o_ref.dtype)


def _ffn_up(hext, w_up, l, cw, cb, has_prev, has_next, *, n_rows, tm, tf):
    D = hext.shape[2]
    F = w_up.shape[2] // 2
    nf = F // tf
    cb2 = cb.reshape(1, 2 * F)
    return pl.pallas_call(
        functools.partial(_ffn_up_kernel, tm=tm),
        out_shape=jax.ShapeDtypeStruct((n_rows, F), bf16),
        grid=(nf, n_rows // tm),
        in_specs=[pl.BlockSpec((None, tm + 2 * HALO, D), lambda j, i: (i, 0, 0)),
                  pl.BlockSpec((None, D, tf), lambda j, i: (l, 0, j)),
                  pl.BlockSpec((None, D, tf), lambda j, i: (l, 0, nf + j)),
                  pl.BlockSpec((3, tf), lambda j, i: (0, j)),
                  pl.BlockSpec((3, tf), lambda j, i: (0, nf + j)),
                  pl.BlockSpec((1, tf), lambda j, i: (0, j)),
                  pl.BlockSpec((1, tf), lambda j, i: (0, nf + j)),
                  pl.BlockSpec((tm, LANES), lambda j, i: (i, 0)),
                  pl.BlockSpec((tm, LANES), lambda j, i: (i, 0))],
        out_specs=pl.BlockSpec((tm, tf), lambda j, i: (i, j)),
        scratch_shapes=[pltpu.VMEM((D, tf), bf16), pltpu.VMEM((D, tf), bf16)],
        compiler_params=_cp("arbitrary", "arbitrary"),
        name="ffn_up",
    )(hext, w_up, w_up, cw, cw, cb2, cb2, has_prev, has_next)


def _sequence_edge_masks(n_lat, n_ctx, S, CTX):
    row = jnp.arange(n_lat + n_ctx, dtype=jnp.int32)
    is_lat = row < n_lat
    pos = jnp.where(is_lat, row % S, (row - n_lat) % CTX)
    last = jnp.where(is_lat, S - 1, CTX - 1)
    has_prev = jnp.broadcast_to((pos != 0).astype(f32)[:, None], (n_lat + n_ctx, LANES))
    has_next = jnp.broadcast_to((pos != last).astype(f32)[:, None], (n_lat + n_ctx, LANES))
    return has_prev, has_next


def _rope_tables(S, tm):
    t = jnp.arange(S, dtype=jnp.int32)
    row = (t // GRID_W).astype(f32)
    col = (t % GRID_W).astype(f32)
    half = HEAD_DIM // 2
    inv = ROPE_THETA ** (-jnp.arange(0, half, 2, dtype=f32) / half)
    ang = jnp.concatenate([row[:, None] * inv, col[:, None] * inv], axis=-1)
    cos = jnp.repeat(jnp.cos(ang), 2, axis=-1)
    sin = jnp.repeat(jnp.sin(ang), 2, axis=-1)
    sign = jnp.where(jnp.arange(HEAD_DIM) % 2 == 0, -1.0, 1.0).astype(f32)
    cos_t = jnp.concatenate([cos, jnp.ones((tm, HEAD_DIM), f32)], axis=0)
    sin_t = jnp.concatenate([sin * sign, jnp.zeros((tm, HEAD_DIM), f32)], axis=0)
    return cos_t, sin_t


def kernel(x, c, ctx, c_ctx, ada_w, ada_b, norm1_g, norm2_g, w_in, a_qn_g, a_kn_g, b_dw_w, b_dw_b,
           b_ln_g, b_ln_b, b_pw_w, b_pw_b, c_rpb, w_out, ffn_w_up, ffn_dw_w, ffn_dw_b, ffn_w_down, final_g):
    B, S, D = x.shape
    CTX = ctx.shape[1]
    L = ada_w.shape[0]
    n_slots = D // HEAD_DIM
    a_heads = 3 * n_slots // 8
    a_kv = a_heads // 3
    c_heads = 3 * n_slots // 8
    b_ch = D - (a_heads + c_heads) * HEAD_DIM
    a_q, a_kvw, c_w = a_heads * HEAD_DIM, a_kv * HEAD_DIM, c_heads * HEAD_DIM
    F = ffn_w_down.shape[1]
    n_lat, n_ctx = B * S, B * CTX
    T = n_lat + n_ctx
    rows = S // GRID_W
    assert B + 1 <= MOD_ROWS and rows % NA_ROWS == 0 and rows >= 2 * NA_ROWS
    assert S % CTX == 0 and n_lat % (NA_ROWS * GRID_W) == 0
    assert S & (S - 1) == 0 and CTX & (CTX - 1) == 0

    tm = _tile(n_ctx, 1024, CTX)
    assert n_lat % tm == 0 and S % tm == 0
    tm_down = _tile(tm, 512, CTX)
    tm_conv = _tile(CTX, 256, LANES)
    tn = _tile(D, 512, LANES)
    tf = _tile(F, 256, LANES)
    tq = _tile(S, 256, LANES)
    kchunk = _tile(S, 512, LANES)
    tq_ctx = _tile(CTX, 256, LANES)

    X = jnp.concatenate([x.reshape(n_lat, D), ctx.reshape(n_ctx, D)], axis=0)
    cvec = jnp.concatenate([c, c_ctx[None, :], jnp.zeros((MOD_ROWS - B - 1, D), f32)], axis=0)
    mod = _ada_mod(cvec, ada_w, ada_b)
    cos_t, sin_t = _rope_tables(S, tm)
    has_prev, has_next = _sequence_edge_masks(n_lat, n_ctx, S, CTX)
    qk_w = a_q + a_kvw
    col_v, col_bu, col_n = qk_w, qk_w + a_kvw, qk_w + a_kvw + 2 * b_ch

    for l in range(L):
        ctx_needed = l < L - 1
        n_rows = T if ctx_needed else n_lat
        m = mod[l].reshape(MOD_ROWS, N_MOD, 1, D)
        sh1, sc1, g1, sh2, sc2, g2 = (m[:, k] for k in range(N_MOD))
        common = dict(n_lat=n_lat, S=S, B=B)

        h = _norm_mod(X, norm1_g[l], sh1, sc1, tm=tm_conv, **common)
        gains = jnp.concatenate([jnp.tile(a_qn_g[l], a_heads), jnp.tile(a_kn_g[l], a_kv)]).reshape(1, qk_w)
        qk = _proj_qk(h, w_in, l, gains, cos_t, sin_t, qk_w, tm=tm, n_lat=n_lat, S=S)
        va = _proj(h, w_in, l, col_v, a_kvw, bf16, tm=tm, tn_pref=512, name="proj_v")
        bu = _proj(h, w_in, l, col_bu, 2 * b_ch, f32, tm=tm, tn_pref=512, name="proj_bu")
        nqkv = _proj(h, w_in, l, col_n, 3 * c_w, bf16, tm=tm, tn_pref=512, name="proj_n")

        o_a = _gqa_latent(qk, va, B=B, S=S, CTX=CTX, heads=a_heads, group=a_heads // a_kv, q_cols=a_q,
                          out_rows=n_rows, tq=tq, kchunk=kchunk)
        bias = _na_bias(c_rpb[l], rows=rows)
        o_c = _na_attention(nqkv, bias, B=B, S=S, CTX=CTX, heads=c_heads, out_rows=n_rows)
        if ctx_needed:
            o_a = _context_attention(qk, (qk, a_q, va, 0, n_lat, CTX), o_a, nb=B, heads=a_heads,
                                     group=a_heads // a_kv, q_row0=n_lat, q_len=CTX, tq=tq_ctx,
                                     name="gqa_context")
            o_c = _context_attention(nqkv, (nqkv, c_w, nqkv, 2 * c_w, n_lat, CTX), o_c, nb=B, heads=c_heads,
                                     group=1, q_row0=n_lat, q_len=CTX, tq=tq_ctx, name="mha_context")
        o_b = _conformer(bu, b_dw_w[l], b_dw_b[l], b_ln_g[l], b_ln_b[l], b_pw_w[l], b_pw_b[l],
                         n_rows=n_rows, n_lat=n_lat, S=S, CTX=CTX, tm=tm_conv)
        X = _proj_residual([o_a, o_b, o_c], w_out, l, X, g1, n_rows=n_rows, tm=tm, tn=tn, name="proj_out",
                           **common)

        h2 = _norm_mod_halo(X, norm2_g[l], sh2, sc2, tm=tm, sub=tm_conv, **common)
        act = _ffn_up(h2, ffn_w_up, l, ffn_dw_w[l], ffn_dw_b[l], has_prev, has_next, n_rows=n_rows,
                      tm=tm, tf=tf)
        X = _proj_residual([act], ffn_w_down, l, X, g2, n_rows=n_rows, tm=tm_down, tn=tn, name="ffn_down",
                           **common)

    out = _final_norm(X, final_g, rows=n_lat, tm=_tile(n_lat, 256, 8))
    return out.reshape(B, S, D)
```

```python
import functools
import math

import jax
import jax.numpy as jnp
from jax import lax
from jax.experimental import pallas as pl
from jax.experimental.pallas import tpu as pltpu

HEAD_DIM = 128
GRID_W = 64
GRID_SHIFT = 6
NA_ROWS = 8
NA_COLS = 16
ROPE_THETA = 10000.0
EPS = 1e-6
N_MOD = 6
NEG_INF = -1e30
LANES = 128
SUBLANES = 8

V7X_VMEM_LIMIT_BYTES = 56 * 1024 * 1024
MOD_ROWS = 8
HALO = 16
ROW_CHUNKS = 2

f32 = jnp.float32
bf16 = jnp.bfloat16
_NT = (((1,), (1,)), ((), ()))


def _cp(*sem):
    return pltpu.CompilerParams(dimension_semantics=sem, vmem_limit_bytes=V7X_VMEM_LIMIT_BYTES)


def _tile(n, pref, unit):
    t = (min(pref, n) // unit) * unit
    while n % t:
        t -= unit
    return t


def _sigmoid(x):
    return 1.0 / (1.0 + jnp.exp(-x))


def _mod_row(i, tm, n_lat, S, B):
    return jnp.where(i * tm < n_lat, (i * tm) // S, B)


def _cast_weights_once(w_ref, wb_ref):
    @pl.when(pl.program_id(1) == 0)
    def _():
        wb_ref[...] = w_ref[...].astype(bf16)


def _mod_kernel(c_ref, w_ref, b_ref, o_ref):
    c = c_ref[...]
    sc = (c * _sigmoid(c)).astype(bf16)
    o_ref[...] = jnp.dot(sc, w_ref[...].astype(bf16), preferred_element_type=f32) + b_ref[...]


def _ada_mod(cvec, ada_w, ada_b):
    L, D, N = ada_w.shape
    tn = _tile(N, 512, LANES)
    return pl.pallas_call(
        _mod_kernel,
        out_shape=jax.ShapeDtypeStruct((L, MOD_ROWS, N), f32),
        grid=(L, N // tn),
        in_specs=[pl.BlockSpec((MOD_ROWS, D), lambda l, j: (0, 0)),
                  pl.BlockSpec((None, D, tn), lambda l, j: (l, 0, j)),
                  pl.BlockSpec((None, 1, tn), lambda l, j: (l, 0, j))],
        out_specs=pl.BlockSpec((None, MOD_ROWS, tn), lambda l, j: (l, 0, j)),
        compiler_params=_cp("arbitrary", "arbitrary"),
        name="ada_mod",
    )(cvec, ada_w, ada_b.reshape(L, 1, N))


def _rms_mod(x, g, sh, sc):
    y = x * lax.rsqrt(jnp.mean(x * x, axis=-1, keepdims=True) + EPS) * g
    return y * (1.0 + sc) + sh


def _norm_mod_kernel(x_ref, g_ref, sh_ref, sc_ref, o_ref):
    o_ref[...] = _rms_mod(x_ref[...], g_ref[...], sh_ref[...], sc_ref[...]).astype(o_ref.dtype)


def _norm_mod_halo_kernel(prev_ref, x_ref, next_ref, g_ref, sh_ref, sc_ref, o_ref, *, per):
    s = pl.program_id(1)
    sub = x_ref.shape[0]
    g, sh, sc = g_ref[...], sh_ref[...], sc_ref[...]
    o_ref[pl.ds(pl.multiple_of(HALO + s * sub, HALO), sub)] = _rms_mod(x_ref[...], g, sh, sc).astype(o_ref.dtype)

    @pl.when(s == 0)
    def _():
        o_ref[0:HALO] = _rms_mod(prev_ref[...], g, sh, sc).astype(o_ref.dtype)

    @pl.when(s == per - 1)
    def _():
        o_ref[HALO + per * sub:2 * HALO + per * sub] = _rms_mod(next_ref[...], g, sh, sc).astype(o_ref.dtype)


def _norm_kernel(x_ref, g_ref, o_ref):
    x = x_ref[...]
    o_ref[...] = x * lax.rsqrt(jnp.mean(x * x, axis=-1, keepdims=True) + EPS) * g_ref[...]


def _norm_mod(X, g, sh, sc, *, n_lat, S, B, tm):
    T, D = X.shape
    mrow = lambda i: (_mod_row(i, tm, n_lat, S, B), 0, 0)
    return pl.pallas_call(
        _norm_mod_kernel,
        out_shape=jax.ShapeDtypeStruct((T, D), bf16),
        grid=(T // tm,),
        in_specs=[pl.BlockSpec((tm, D), lambda i: (i, 0)),
                  pl.BlockSpec((1, D), lambda i: (0, 0)),
                  pl.BlockSpec((None, 1, D), mrow),
                  pl.BlockSpec((None, 1, D), mrow)],
        out_specs=pl.BlockSpec((tm, D), lambda i: (i, 0)),
        compiler_params=_cp("arbitrary"),
        name="norm_mod",
    )(X, g.reshape(1, D), sh, sc)


def _norm_mod_halo(X, g, sh, sc, *, n_lat, S, B, tm, sub):
    T, D = X.shape
    assert tm % sub == 0
    hb = tm // HALO
    n_hb = T // HALO
    per = tm // sub
    mrow = lambda i, s: (_mod_row(i, tm, n_lat, S, B), 0, 0)
    return pl.pallas_call(
        functools.partial(_norm_mod_halo_kernel, per=per),
        out_shape=jax.ShapeDtypeStruct((T // tm, tm + 2 * HALO, D), bf16),
        grid=(T // tm, per),
        in_specs=[pl.BlockSpec((HALO, D), lambda i, s: (jnp.maximum(i * hb - 1, 0), 0)),
                  pl.BlockSpec((sub, D), lambda i, s: (i * per + s, 0)),
                  pl.BlockSpec((HALO, D), lambda i, s: (jnp.minimum((i + 1) * hb, n_hb - 1), 0)),
                  pl.BlockSpec((1, D), lambda i, s: (0, 0)),
                  pl.BlockSpec((None, 1, D), mrow),
                  pl.BlockSpec((None, 1, D), mrow)],
        out_specs=pl.BlockSpec((None, tm + 2 * HALO, D), lambda i, s: (i, 0, 0)),
        compiler_params=_cp("arbitrary", "arbitrary"),
        name="norm_mod_halo",
    )(X, X, X, g.reshape(1, D), sh, sc)


def _final_norm(X, g, *, rows, tm):
    D = X.shape[1]
    return pl.pallas_call(
        _norm_kernel,
        out_shape=jax.ShapeDtypeStruct((rows, D), f32),
        grid=(rows // tm,),
        in_specs=[pl.BlockSpec((tm, D), lambda i: (i, 0)),
                  pl.BlockSpec((1, D), lambda i: (0, 0))],
        out_specs=pl.BlockSpec((tm, D), lambda i: (i, 0)),
        compiler_params=_cp("arbitrary"),
        name="final_norm",
    )(X, g.reshape(1, D))


def _mm_kernel(h_ref, w_ref, o_ref, wb_ref):
    _cast_weights_once(w_ref, wb_ref)
    o_ref[...] = jnp.dot(h_ref[...], wb_ref[...], preferred_element_type=f32).astype(o_ref.dtype)


def _proj(h, w, l, col0, ncols, out_dtype, *, tm, tn_pref, name):
    T, D = h.shape
    tn = _tile(ncols, tn_pref, LANES)
    while col0 % tn or ncols % tn:
        tn -= LANES
    c0 = col0 // tn
    return pl.pallas_call(
        _mm_kernel,
        out_shape=jax.ShapeDtypeStruct((T, ncols), out_dtype),
        grid=(ncols // tn, T // tm),
        in_specs=[pl.BlockSpec((tm, D), lambda j, i: (i, 0)),
                  pl.BlockSpec((None, D, tn), lambda j, i: (l, 0, c0 + j))],
        out_specs=pl.BlockSpec((tm, tn), lambda j, i: (i, j)),
        scratch_shapes=[pltpu.VMEM((D, tn), bf16)],
        compiler_params=_cp("arbitrary", "arbitrary"),
        name=name,
    )(h, w)


def _mm_qk_kernel(h_ref, w_ref, g_ref, cos_ref, sin_ref, o_ref, wb_ref):
    _cast_weights_once(w_ref, wb_ref)
    tm, tn = o_ref.shape
    n_chunks = 2 * ROW_CHUNKS
    rc = tm // n_chunks
    even = (lax.broadcasted_iota(jnp.int32, (rc, HEAD_DIM), 1) & 1) == 0
    for r in range(n_chunks):
        rows = slice(r * rc, (r + 1) * rc)
        acc = jnp.dot(h_ref[rows, :], wb_ref[...], preferred_element_type=f32)
        cosv = cos_ref[rows, :]
        sinv = sin_ref[rows, :]
        for hh in range(tn // HEAD_DIM):
            sl = slice(hh * HEAD_DIM, (hh + 1) * HEAD_DIM)
            blk = acc[:, sl]
            y = blk * lax.rsqrt(jnp.mean(blk * blk, axis=-1, keepdims=True) + EPS) * g_ref[:, sl]
            partner = jnp.where(even, pltpu.roll(y, HEAD_DIM - 1, 1), pltpu.roll(y, 1, 1))
            o_ref[rows, sl] = (y * cosv + partner * sinv).astype(o_ref.dtype)


def _proj_qk(h, w, l, gains, cos_t, sin_t, ncols, *, tm, n_lat, S):
    T, D = h.shape
    tn = _tile(ncols, 512, LANES)
    pos_blocks = S // tm

    def pos(j, i):
        return (jnp.where(i * tm < n_lat, i % pos_blocks, pos_blocks), 0)

    return pl.pallas_call(
        _mm_qk_kernel,
        out_shape=jax.ShapeDtypeStruct((T, ncols), bf16),
        grid=(ncols // tn, T // tm),
        in_specs=[pl.BlockSpec((tm, D), lambda j, i: (i, 0)),
                  pl.BlockSpec((None, D, tn), lambda j, i: (l, 0, j)),
                  pl.BlockSpec((1, tn), lambda j, i: (0, j)),
                  pl.BlockSpec((tm, HEAD_DIM), pos),
                  pl.BlockSpec((tm, HEAD_DIM), pos)],
        out_specs=pl.BlockSpec((tm, tn), lambda j, i: (i, j)),
        scratch_shapes=[pltpu.VMEM((D, tn), bf16)],
        compiler_params=_cp("arbitrary", "arbitrary"),
        name="proj_qk",
    )(h, w, gains, cos_t, sin_t)


def _lane_max(s):
    mx = s[:, :LANES]
    for t in range(1, s.shape[1] // LANES):
        mx = jnp.maximum(mx, s[:, t * LANES:(t + 1) * LANES])
    return mx


def _attn_kernel(*refs, n_seg, group, scale):
    q_ref = refs[0]
    kv = refs[1:1 + 2 * n_seg]
    o_ref = refs[1 + 2 * n_seg]
    for g in range(group):
        sl = slice(g * HEAD_DIM, (g + 1) * HEAD_DIM)
        q = q_ref[:, sl]
        ss = [lax.dot_general(q, kv[2 * i][...], _NT, preferred_element_type=f32) * scale
              for i in range(n_seg)]
        m = jnp.max(ss[0], axis=-1, keepdims=True)
        for s in ss[1:]:
            m = jnp.maximum(m, jnp.max(s, axis=-1, keepdims=True))
        l = None
        o = None
        for i, s in enumerate(ss):
            p = jnp.exp(s - m)
            li = jnp.sum(p, axis=-1, keepdims=True)
            oi = jnp.dot(p.astype(bf16), kv[2 * i + 1][...], preferred_element_type=f32)
            l = li if l is None else l + li
            o = oi if o is None else o + oi
        o_ref[:, sl] = (o / l).astype(o_ref.dtype)


def _context_attention(q_arr, kv_seg, *, nb, heads, group, q_row0, q_len, tq, name):
    gw = group * HEAD_DIM
    n_kv = heads // group
    nq = q_len // tq
    qb0 = q_row0 // tq
    k_arr, k_col0, v_arr, v_col0, row0, length = kv_seg
    rb0, kc0, vc0 = row0 // length, k_col0 // HEAD_DIM, v_col0 // HEAD_DIM
    return pl.pallas_call(
        functools.partial(_attn_kernel, n_seg=1, group=group, scale=HEAD_DIM ** -0.5),
        out_shape=jax.ShapeDtypeStruct((nb * q_len, heads * HEAD_DIM), bf16),
        grid=(n_kv, nb, nq),
        in_specs=[pl.BlockSpec((tq, gw), lambda j, b, i: (qb0 + b * nq + i, j)),
                  pl.BlockSpec((length, HEAD_DIM), lambda j, b, i: (rb0 + b, kc0 + j)),
                  pl.BlockSpec((length, HEAD_DIM), lambda j, b, i: (rb0 + b, vc0 + j))],
        out_specs=pl.BlockSpec((tq, gw), lambda j, b, i: (b * nq + i, j)),
        compiler_params=_cp("arbitrary", "arbitrary", "arbitrary"),
        name=name,
    )(q_arr, k_arr, v_arr)


def _gqa_kernel(q_ref, kc_ref, vc_ref, kl_ref, vl_ref, o_ref,
                qs_ref, s_ref, sc_ref, m_ref, l_ref, acc_ref, *, group, kchunk, scale, unroll):
    tq = q_ref.shape[0]
    for g in range(group):
        qs_ref[g * tq:(g + 1) * tq, :] = q_ref[:, g * HEAD_DIM:(g + 1) * HEAD_DIM]
    q = qs_ref[...]
    n_chunks = kl_ref.shape[0] // kchunk
    c1 = scale * math.log2(math.e)

    s0 = lax.dot_general(q, kc_ref[...], _NT, preferred_element_type=f32) * c1
    sc_ref[...] = s0
    m_ref[...] = _lane_max(s0)

    def pass1(c, carry):
        k = kl_ref[pl.ds(pl.multiple_of(c * kchunk, kchunk), kchunk), :]
        s = lax.dot_general(q, k, _NT, preferred_element_type=f32) * c1
        s_ref[c] = s
        m_ref[...] = jnp.maximum(m_ref[...], _lane_max(s))
        return carry

    lax.fori_loop(0, n_chunks, pass1, 0, unroll=unroll)
    m_ref[...] = jnp.broadcast_to(jnp.max(m_ref[...], axis=-1, keepdims=True), m_ref.shape)

    def probs(s):
        mb = m_ref[...]
        ps = [jnp.exp2(s[:, t * LANES:(t + 1) * LANES] - mb) for t in range(s.shape[1] // LANES)]
        tot = ps[0]
        for p in ps[1:]:
            tot = tot + p
        return jnp.concatenate(ps, axis=1).astype(bf16), tot

    p0, t0 = probs(sc_ref[...])
    l_ref[...] = t0
    acc_ref[...] = jnp.dot(p0, vc_ref[...], preferred_element_type=f32)

    def pass2(c, carry):
        p, tot = probs(s_ref[c])
        l_ref[...] += tot
        v = vl_ref[pl.ds(pl.multiple_of(c * kchunk, kchunk), kchunk), :]
        acc_ref[...] += jnp.dot(p, v, preferred_element_type=f32)
        return carry

    lax.fori_loop(0, n_chunks, pass2, 0, unroll=unroll)
    o = acc_ref[...] / jnp.sum(l_ref[...], axis=-1, keepdims=True)
    for g in range(group):
        o_ref[:, g * HEAD_DIM:(g + 1) * HEAD_DIM] = o[g * tq:(g + 1) * tq].astype(o_ref.dtype)


def _gqa_latent(qk, va, *, B, S, CTX, heads, group, q_cols, out_rows, tq, kchunk):
    gw = group * HEAD_DIM
    n_kv = heads // group
    nq = S // tq
    kc0 = q_cols // HEAD_DIM
    ctx_b0 = (B * S) // CTX
    M = group * tq
    return pl.pallas_call(
        functools.partial(_gqa_kernel, group=group, kchunk=kchunk, scale=HEAD_DIM ** -0.5, unroll=4),
        out_shape=jax.ShapeDtypeStruct((out_rows, q_cols), bf16),
        grid=(n_kv, B, nq),
        in_specs=[pl.BlockSpec((tq, gw), lambda j, b, i: (b * nq + i, j)),
                  pl.BlockSpec((CTX, HEAD_DIM), lambda j, b, i: (ctx_b0 + b, kc0 + j)),
                  pl.BlockSpec((CTX, HEAD_DIM), lambda j, b, i: (ctx_b0 + b, j)),
                  pl.BlockSpec((S, HEAD_DIM), lambda j, b, i: (b, kc0 + j)),
                  pl.BlockSpec((S, HEAD_DIM), lambda j, b, i: (b, j))],
        out_specs=pl.BlockSpec((tq, gw), lambda j, b, i: (b * nq + i, j)),
        scratch_shapes=[pltpu.VMEM((M, HEAD_DIM), bf16),
                        pltpu.VMEM((S // kchunk, M, kchunk), f32),
                        pltpu.VMEM((M, CTX), f32),
                        pltpu.VMEM((M, LANES), f32),
                        pltpu.VMEM((M, LANES), f32),
                        pltpu.VMEM((M, HEAD_DIM), f32)],
        compiler_params=_cp("arbitrary", "arbitrary", "arbitrary"),
        name="gqa_latent",
    )(qk, qk, va, qk, va)


def _na_bias_kernel(rpb_ref, o_ref, tw_ref, *, rows):
    h = pl.program_id(0)
    case = pl.program_id(1)
    n_r = 2 * NA_ROWS - 1
    n_c = 2 * NA_COLS - 1
    per_head = n_r * n_c
    shape = (GRID_W, 2 * GRID_W)
    qc = lax.broadcasted_iota(jnp.int32, shape, 0)
    lane = lax.broadcasted_iota(jnp.int32, shape, 1)
    kc = lane & (GRID_W - 1)
    second = lane >= GRID_W
    dc = kc - qc
    cs = jnp.clip(qc - NA_COLS // 2, 0, GRID_W - NA_COLS)
    col_ok = (kc >= cs) & (kc < cs + NA_COLS)
    neg = jnp.full(shape, NEG_INF, f32)
    lo, hi = -NA_ROWS, NA_ROWS - 1
    span = 2 * NA_ROWS - 1
    for a in range(-span, span):
        if lo <= a <= hi:
            acc = neg
            for bb in range(n_c):
                s0 = rpb_ref[h * per_head + (a + NA_ROWS - 1) * n_c + bb] if -NA_ROWS < a < NA_ROWS else NEG_INF
                s1 = rpb_ref[h * per_head + (a + NA_ROWS) * n_c + bb] if -NA_ROWS < a + 1 < NA_ROWS else NEG_INF
                acc = jnp.where(dc == bb - (NA_COLS - 1), jnp.where(second, s1, s0), acc)
            tw_ref[a + span] = jnp.where(col_ok, acc, NEG_INF)
        else:
            tw_ref[a + span] = neg
    qbase = jnp.where(case == 0, 0, jnp.where(case == 1, NA_ROWS, rows - NA_ROWS))
    kbase = jnp.where(case == 0, 0, jnp.where(case == 1, NA_ROWS // 2, rows - 2 * NA_ROWS))
    off = kbase - qbase
    pair_w = 2 * GRID_W
    chunk_w = 2 * pair_w
    for qr in range(NA_ROWS):
        for kp in range(NA_ROWS):
            a = 2 * kp - qr + off
            lane0 = (kp % 2) * pair_w
            o_ref[kp // 2, qr * GRID_W:(qr + 1) * GRID_W, lane0:lane0 + pair_w] = tw_ref[a + span]
    full = (NA_ROWS * GRID_W, chunk_w)
    qrow = qbase + (lax.broadcasted_iota(jnp.int32, full, 0) >> GRID_SHIFT)
    rs = jnp.clip(qrow - NA_ROWS // 2, 0, rows - NA_ROWS)
    for ch in range(4):
        krow = kbase + 4 * ch + (lax.broadcasted_iota(jnp.int32, full, 1) >> GRID_SHIFT)
        valid = (krow >= rs) & (krow < rs + NA_ROWS)
        o_ref[ch] = jnp.where(valid, o_ref[ch], NEG_INF)


def _na_bias(rpb_l, *, rows):
    H = rpb_l.shape[0]
    tq, kb = NA_ROWS * GRID_W, 4 * GRID_W
    return pl.pallas_call(
        functools.partial(_na_bias_kernel, rows=rows),
        out_shape=jax.ShapeDtypeStruct((H, 3, 4, tq, kb), f32),
        grid=(H, 3),
        in_specs=[pl.BlockSpec(memory_space=pltpu.SMEM)],
        out_specs=pl.BlockSpec((None, None, 4, tq, kb), lambda h, c: (h, c, 0, 0, 0)),
        scratch_shapes=[pltpu.VMEM((2 * (2 * NA_ROWS - 1), GRID_W, 2 * GRID_W), f32)],
        compiler_params=_cp("arbitrary", "arbitrary"),
        name="na_bias",
    )(rpb_l.reshape(-1))


def _na_kernel(q_ref, k0, k1, k2, k3, v0, v1, v2, v3, kc_ref, vc_ref, bias_ref, o_ref,
               s_ref, sc_ref, m_ref, l_ref, acc_ref, *, scale, hp):
    ks = (k0, k1, k2, k3)
    vs = (v0, v1, v2, v3)
    for hh in range(hp):
        sl = slice(hh * HEAD_DIM, (hh + 1) * HEAD_DIM)
        q = q_ref[:, sl]
        s0 = lax.dot_general(q, kc_ref[:, sl], _NT, preferred_element_type=f32) * scale
        sc_ref[hh] = s0
        m_ref[hh] = _lane_max(s0)
        for i in range(4):
            s = lax.dot_general(q, ks[i][:, sl], _NT, preferred_element_type=f32) * scale + bias_ref[hh, i]
            s_ref[hh, i] = s
            m_ref[hh] = jnp.maximum(m_ref[hh], _lane_max(s))
        m_ref[hh] = jnp.broadcast_to(jnp.max(m_ref[hh], axis=-1, keepdims=True), m_ref.shape[1:])

        def probs(s):
            mb = m_ref[hh]
            ps = [jnp.exp(s[:, t * LANES:(t + 1) * LANES] - mb) for t in range(s.shape[1] // LANES)]
            tot = ps[0]
            for p in ps[1:]:
                tot = tot + p
            return jnp.concatenate(ps, axis=1).astype(bf16), tot

        p0, t0 = probs(sc_ref[hh])
        l_ref[hh] = t0
        acc_ref[hh] = jnp.dot(p0, vc_ref[:, sl], preferred_element_type=f32)
        for i in range(4):
            p, tot = probs(s_ref[hh, i])
            l_ref[hh] += tot
            acc_ref[hh] += jnp.dot(p, vs[i][:, sl], preferred_element_type=f32)
        o_ref[:, sl] = (acc_ref[hh] / jnp.sum(l_ref[hh], axis=-1, keepdims=True)).astype(o_ref.dtype)


def _na_attention(nqkv, bias, *, B, S, CTX, heads, out_rows):
    rows = S // GRID_W
    tq = NA_ROWS * GRID_W
    kb = tq // 2
    nt = rows // NA_ROWS
    kb_per_seq = S // kb
    ctx_b0 = (B * S) // CTX

    def case(t):
        return jnp.where(t == 0, 0, jnp.where(t == nt - 1, 2, 1))

    def kblock(t):
        return jnp.clip(2 * t - 1, 0, kb_per_seq - 4)

    hp = next(n for n in (6, 4, 3, 2, 1) if heads % n == 0)
    hw = hp * HEAD_DIM
    hblocks = heads // hp

    def kspec(col0, jj):
        return pl.BlockSpec((kb, hw), lambda h, t, b: (b * kb_per_seq + kblock(t) + jj, col0 + h))

    in_specs = [pl.BlockSpec((tq, hw), lambda h, t, b: (b * nt + t, h))]
    in_specs += [kspec(hblocks, jj) for jj in range(4)]
    in_specs += [kspec(2 * hblocks, jj) for jj in range(4)]
    in_specs += [pl.BlockSpec((CTX, hw), lambda h, t, b: (ctx_b0 + b, hblocks + h)),
                 pl.BlockSpec((CTX, hw), lambda h, t, b: (ctx_b0 + b, 2 * hblocks + h)),
                 pl.BlockSpec((hp, None, 4, tq, kb), lambda h, t, b: (h, case(t), 0, 0, 0))]
    return pl.pallas_call(
        functools.partial(_na_kernel, scale=HEAD_DIM ** -0.5, hp=hp),
        out_shape=jax.ShapeDtypeStruct((out_rows, heads * HEAD_DIM), bf16),
        grid=(hblocks, nt, B),
        in_specs=in_specs,
        out_specs=pl.BlockSpec((tq, hw), lambda h, t, b: (b * nt + t, h)),
        scratch_shapes=[pltpu.VMEM((hp, 4, tq, kb), f32),
                        pltpu.VMEM((hp, tq, CTX), f32),
                        pltpu.VMEM((hp, tq, LANES), f32),
                        pltpu.VMEM((hp, tq, LANES), f32),
                        pltpu.VMEM((hp, tq, HEAD_DIM), f32)],
        compiler_params=_cp("arbitrary", "arbitrary", "arbitrary"),
        name="na_attn",
    )(*([nqkv] * 11), bias)


def _conf_kernel(prev_ref, cur_ref, next_ref, dww_ref, dwb_ref, lng_ref, lnb_ref, pww_ref, pwb_ref,
                 o_ref, z_ref, zs_ref, y_ref, pwb16_ref, *, tm, ch, n_taps, lat_tiles, lat_tiles_per_seq,
                 ctx_tiles_per_seq):
    i = pl.program_id(0)

    @pl.when(i == 0)
    def _():
        pwb16_ref[...] = pww_ref[...].astype(bf16)

    is_lat = i < lat_tiles
    per_seq = jnp.where(is_lat, lat_tiles_per_seq, ctx_tiles_per_seq)
    local = jnp.where(is_lat, i, i - lat_tiles) % per_seq
    first = local == 0
    last = local == per_seq - 1

    def glu(u):
        return u[:, :ch] * _sigmoid(u[:, ch:])

    z_ref[0:HALO] = jnp.where(first, 0.0, glu(prev_ref[...]))
    z_ref[HALO:HALO + tm] = glu(cur_ref[...])
    z_ref[HALO + tm:2 * HALO + tm] = jnp.where(last, 0.0, glu(next_ref[...]))
    half = (n_taps - 1) // 2
    n_sh = tm + 2 * HALO - SUBLANES
    for b in range(1, SUBLANES):
        zs_ref[b - 1, 0:n_sh, :] = z_ref[b:b + n_sh, :]
    rb = 128
    for cb in range(ch // LANES):
        cols = slice(cb * LANES, (cb + 1) * LANES)
        for r in range(tm // rb):
            acc = jnp.zeros((rb, LANES), f32)
            for k in range(n_taps):
                a, b = divmod(HALO - half + k, SUBLANES)
                r0 = r * rb + a * SUBLANES
                win = z_ref[r0:r0 + rb, cols] if b == 0 else zs_ref[b - 1, r0:r0 + rb, cols]
                acc = acc + dww_ref[k:k + 1, cols] * win
            y_ref[r * rb:(r + 1) * rb, cols] = acc + dwb_ref[:, cols]
    y = y_ref[...]
    mu = jnp.mean(y, axis=-1, keepdims=True)
    d = y - mu
    var = jnp.mean(d * d, axis=-1, keepdims=True)
    yn = d * lax.rsqrt(var + EPS) * lng_ref[...] + lnb_ref[...]
    act = (yn * _sigmoid(yn)).astype(bf16)
    o_ref[...] = (jnp.dot(act, pwb16_ref[...], preferred_element_type=f32) + pwb_ref[...]).astype(o_ref.dtype)


def _conformer(bu, dww, dwb, lng, lnb, pww, pwb, *, n_rows, n_lat, S, CTX, tm):
    T = bu.shape[0]
    ch = bu.shape[1] // 2
    n_taps = dww.shape[0]
    hb = tm // HALO
    n_hb = T // HALO
    kern = functools.partial(_conf_kernel, tm=tm, ch=ch, n_taps=n_taps, lat_tiles=n_lat // tm,
                             lat_tiles_per_seq=S // tm, ctx_tiles_per_seq=CTX // tm)
    row = lambda i: (0, 0)
    return pl.pallas_call(
        kern,
        out_shape=jax.ShapeDtypeStruct((n_rows, ch), bf16),
        grid=(n_rows // tm,),
        in_specs=[pl.BlockSpec((HALO, 2 * ch), lambda i: (jnp.maximum(i * hb - 1, 0), 0)),
                  pl.BlockSpec((tm, 2 * ch), lambda i: (i, 0)),
                  pl.BlockSpec((HALO, 2 * ch), lambda i: (jnp.minimum((i + 1) * hb, n_hb - 1), 0)),
                  pl.BlockSpec((n_taps, ch), row),
                  pl.BlockSpec((1, ch), row), pl.BlockSpec((1, ch), row), pl.BlockSpec((1, ch), row),
                  pl.BlockSpec((ch, ch), row), pl.BlockSpec((1, ch), row)],
        out_specs=pl.BlockSpec((tm, ch), lambda i: (i, 0)),
        scratch_shapes=[pltpu.VMEM((tm + 2 * HALO, ch), f32),
                        pltpu.VMEM((SUBLANES - 1, tm + 2 * HALO, ch), f32), pltpu.VMEM((tm, ch), f32),
                        pltpu.VMEM((ch, ch), bf16)],
        compiler_params=_cp("arbitrary"),
        name="conformer",
    )(bu, bu, bu, dww, dwb.reshape(1, ch), lng.reshape(1, ch), lnb.reshape(1, ch), pww, pwb.reshape(1, ch))


def _mm_res_kernel(*refs, splits):
    n = len(splits)
    w_ref, x_ref, g_ref, o_ref, wb_ref = refs[n:]
    _cast_weights_once(w_ref, wb_ref)
    acc = None
    off = 0
    for r, k in zip(refs[:n], splits):
        part = jnp.dot(r[...], wb_ref[off:off + k, :], preferred_element_type=f32)
        acc = part if acc is None else acc + part
        off += k
    o_ref[...] = x_ref[...] + g_ref[...] * acc


def _proj_residual(lhs, w, l, X, gate, *, n_rows, n_lat, S, B, tm, tn, name):
    _, K, N = w.shape
    splits = tuple(a.shape[1] for a in lhs)
    in_specs = [pl.BlockSpec((tm, k), lambda j, i: (i, 0)) for k in splits]
    in_specs += [pl.BlockSpec((None, K, tn), lambda j, i: (l, 0, j)),
                 pl.BlockSpec((tm, tn), lambda j, i: (i, j)),
                 pl.BlockSpec((None, 1, tn), lambda j, i: (_mod_row(i, tm, n_lat, S, B), 0, j))]
    return pl.pallas_call(
        functools.partial(_mm_res_kernel, splits=splits),
        out_shape=jax.ShapeDtypeStruct((n_rows, N), f32),
        grid=(N // tn, n_rows // tm),
        in_specs=in_specs,
        out_specs=pl.BlockSpec((tm, tn), lambda j, i: (i, j)),
        scratch_shapes=[pltpu.VMEM((K, tn), bf16)],
        compiler_params=_cp("arbitrary", "arbitrary"),
        name=name,
    )(*lhs, w, X, gate)


def _ffn_up_kernel(h_ref, wg_ref, wv_ref, cwg_ref, cwv_ref, cbg_ref, cbv_ref, mp_ref, mn_ref, o_ref,
                   wgb_ref, wvb_ref, *, tm):
    _cast_weights_once(wg_ref, wgb_ref)
    _cast_weights_once(wv_ref, wvb_ref)
    tf = o_ref.shape[1]
    rc = tm // ROW_CHUNKS
    n_ext = rc + 2 * HALO
    for r in range(ROW_CHUNKS):
        rows = slice(r * rc, (r + 1) * rc)
        h = h_ref[r * rc:r * rc + n_ext, :]
        has_prev = jnp.concatenate([mp_ref[rows, :]] * (tf // LANES), axis=1)
        has_next = jnp.concatenate([mn_ref[rows, :]] * (tf // LANES), axis=1)

        def branch(wb_ref, cw_ref, cb_ref):
            u = jnp.dot(h, wb_ref[...], preferred_element_type=f32)
            up = pltpu.roll(u, 1, 0)[HALO:HALO + rc]
            dn = pltpu.roll(u, n_ext - 1, 0)[HALO:HALO + rc]
            return (cw_ref[0:1, :] * (up * has_prev) + cw_ref[1:2, :] * u[HALO:HALO + rc]
                    + cw_ref[2:3, :] * (dn * has_next) + cb_ref[...])

        gate = branch(wgb_ref, cwg_ref, cbg_ref)
        val = branch(wvb_ref, cwv_ref, cbv_ref)
        o_ref[rows, :] = (gate * _sigmoid(gate) * val).astype(o_ref.dtype)


def _ffn_up(hext, w_up, l, cw, cb, has_prev, has_next, *, n_rows, tm, tf):
    D = hext.shape[2]
    F = w_up.shape[2] // 2
    nf = F // tf
    cb2 = cb.reshape(1, 2 * F)
    return pl.pallas_call(
        functools.partial(_ffn_up_kernel, tm=tm),
        out_shape=jax.ShapeDtypeStruct((n_rows, F), bf16),
        grid=(nf, n_rows // tm),
        in_specs=[pl.BlockSpec((None, tm + 2 * HALO, D), lambda j, i: (i, 0, 0)),
                  pl.BlockSpec((None, D, tf), lambda j, i: (l, 0, j)),
                  pl.BlockSpec((None, D, tf), lambda j, i: (l, 0, nf + j)),
                  pl.BlockSpec((3, tf), lambda j, i: (0, j)),
                  pl.BlockSpec((3, tf), lambda j, i: (0, nf + j)),
                  pl.BlockSpec((1, tf), lambda j, i: (0, j)),
                  pl.BlockSpec((1, tf), lambda j, i: (0, nf + j)),
                  pl.BlockSpec((tm, LANES), lambda j, i: (i, 0)),
                  pl.BlockSpec((tm, LANES), lambda j, i: (i, 0))],
        out_specs=pl.BlockSpec((tm, tf), lambda j, i: (i, j)),
        scratch_shapes=[pltpu.VMEM((D, tf), bf16), pltpu.VMEM((D, tf), bf16)],
        compiler_params=_cp("arbitrary", "arbitrary"),
        name="ffn_up",
    )(hext, w_up, w_up, cw, cw, cb2, cb2, has_prev, has_next)


def _sequence_edge_masks(n_lat, n_ctx, S, CTX):
    row = jnp.arange(n_lat + n_ctx, dtype=jnp.int32)
    is_lat = row < n_lat
    pos = jnp.where(is_lat, row % S, (row - n_lat) % CTX)
    last = jnp.where(is_lat, S - 1, CTX - 1)
    has_prev = jnp.broadcast_to((pos != 0).astype(f32)[:, None], (n_lat + n_ctx, LANES))
    has_next = jnp.broadcast_to((pos != last).astype(f32)[:, None], (n_lat + n_ctx, LANES))
    return has_prev, has_next


def _rope_tables(S, tm):
    t = jnp.arange(S, dtype=jnp.int32)
    row = (t // GRID_W).astype(f32)
    col = (t % GRID_W).astype(f32)
    half = HEAD_DIM // 2
    inv = ROPE_THETA ** (-jnp.arange(0, half, 2, dtype=f32) / half)
    ang = jnp.concatenate([row[:, None] * inv, col[:, None] * inv], axis=-1)
    cos = jnp.repeat(jnp.cos(ang), 2, axis=-1)
    sin = jnp.repeat(jnp.sin(ang), 2, axis=-1)
    sign = jnp.where(jnp.arange(HEAD_DIM) % 2 == 0, -1.0, 1.0).astype(f32)
    cos_t = jnp.concatenate([cos, jnp.ones((tm, HEAD_DIM), f32)], axis=0)
    sin_t = jnp.concatenate([sin * sign, jnp.zeros((tm, HEAD_DIM), f32)], axis=0)
    return cos_t, sin_t


def kernel(x, c, ctx, c_ctx, ada_w, ada_b, norm1_g, norm2_g, w_in, a_qn_g, a_kn_g, b_dw_w, b_dw_b,
           b_ln_g, b_ln_b, b_pw_w, b_pw_b, c_rpb, w_out, ffn_w_up, ffn_dw_w, ffn_dw_b, ffn_w_down, final_g):
    B, S, D = x.shape
    CTX = ctx.shape[1]
    L = ada_w.shape[0]
    n_slots = D // HEAD_DIM
    a_heads = 3 * n_slots // 8
    a_kv = a_heads // 3
    c_heads = 3 * n_slots // 8
    b_ch = D - (a_heads + c_heads) * HEAD_DIM
    a_q, a_kvw, c_w = a_heads * HEAD_DIM, a_kv * HEAD_DIM, c_heads * HEAD_DIM
    F = ffn_w_down.shape[1]
    n_lat, n_ctx = B * S, B * CTX
    T = n_lat + n_ctx
    rows = S // GRID_W
    assert B + 1 <= MOD_ROWS and rows % NA_ROWS == 0 and rows >= 2 * NA_ROWS
    assert S % CTX == 0 and n_lat % (NA_ROWS * GRID_W) == 0
    assert S & (S - 1) == 0 and CTX & (CTX - 1) == 0

    tm = _tile(n_ctx, 1024, CTX)
    assert n_lat % tm == 0 and S % tm == 0
    tm_down = _tile(tm, 512, CTX)
    tm_conv = _tile(CTX, 256, LANES)
    tm_norm = _tile(tm, 512, CTX)
    tn = _tile(D, 512, LANES)
    tf = _tile(F, 256, LANES)
    tq = _tile(S, 512, LANES)
    kchunk = _tile(S, 512, LANES)
    tq_ctx = _tile(CTX, 256, LANES)

    X = jnp.concatenate([x.reshape(n_lat, D), ctx.reshape(n_ctx, D)], axis=0)
    cvec = jnp.concatenate([c, c_ctx[None, :], jnp.zeros((MOD_ROWS - B - 1, D), f32)], axis=0)
    mod = _ada_mod(cvec, ada_w, ada_b)
    cos_t, sin_t = _rope_tables(S, tm)
    has_prev, has_next = _sequence_edge_masks(n_lat, n_ctx, S, CTX)
    qk_w = a_q + a_kvw
    col_v, col_bu, col_n = qk_w, qk_w + a_kvw, qk_w + a_kvw + 2 * b_ch

    for l in range(L):
        ctx_needed = l < L - 1
        n_rows = T if ctx_needed else n_lat
        m = mod[l].reshape(MOD_ROWS, N_MOD, 1, D)
        sh1, sc1, g1, sh2, sc2, g2 = (m[:, k] for k in range(N_MOD))
        common = dict(n_lat=n_lat, S=S, B=B)

        h = _norm_mod(X, norm1_g[l], sh1, sc1, tm=tm_norm, **common)
        gains = jnp.concatenate([jnp.tile(a_qn_g[l], a_heads), jnp.tile(a_kn_g[l], a_kv)]).reshape(1, qk_w)
        qk = _proj_qk(h, w_in, l, gains, cos_t, sin_t, qk_w, tm=tm, n_lat=n_lat, S=S)
        va = _proj(h, w_in, l, col_v, a_kvw, bf16, tm=tm, tn_pref=512, name="proj_v")
        bu = _proj(h, w_in, l, col_bu, 2 * b_ch, f32, tm=tm, tn_pref=512, name="proj_bu")
        nqkv = _proj(h, w_in, l, col_n, 3 * c_w, bf16, tm=tm, tn_pref=512, name="proj_n")

        o_a = _gqa_latent(qk, va, B=B, S=S, CTX=CTX, heads=a_heads, group=a_heads // a_kv, q_cols=a_q,
                          out_rows=n_lat, tq=tq, kchunk=kchunk)
        bias = _na_bias(c_rpb[l], rows=rows)
        o_c = _na_attention(nqkv, bias, B=B, S=S, CTX=CTX, heads=c_heads, out_rows=n_lat)
        if ctx_needed:
            oc_a = _context_attention(qk, (qk, a_q, va, 0, n_lat, CTX), nb=B, heads=a_heads,
                                      group=a_heads // a_kv, q_row0=n_lat, q_len=CTX, tq=tq_ctx,
                                      name="gqa_context")
            oc_c = _context_attention(nqkv, (nqkv, c_w, nqkv, 2 * c_w, n_lat, CTX), nb=B, heads=c_heads,
                                      group=1, q_row0=n_lat, q_len=CTX, tq=tq_ctx, name="mha_context")
            o_a = jnp.concatenate([o_a, oc_a], axis=0)
            o_c = jnp.concatenate([o_c, oc_c], axis=0)
        o_b = _conformer(bu, b_dw_w[l], b_dw_b[l], b_ln_g[l], b_ln_b[l], b_pw_w[l], b_pw_b[l],
                         n_rows=n_rows, n_lat=n_lat, S=S, CTX=CTX, tm=tm_conv)
        X = _proj_residual([o_a, o_b, o_c], w_out, l, X, g1, n_rows=n_rows, tm=tm, tn=tn, name="proj_out",
                           **common)

        h2 = _norm_mod_halo(X, norm2_g[l], sh2, sc2, tm=tm, sub=tm_norm, **common)
        act = _ffn_up(h2, ffn_w_up, l, ffn_dw_w[l], ffn_dw_b[l], has_prev, has_next, n_rows=n_rows,
                      tm=tm, tf=tf)
        X = _proj_residual([act], ffn_w_down, l, X, g2, n_rows=n_rows, tm=tm_down, tn=tn, name="ffn_down",
                           **common)

    out = _final_norm(X, final_g, rows=n_lat, tm=_tile(n_lat, 512, SUBLANES))
    return out.reshape(B, S, D)
```

```python
import functools
import math

import jax
import jax.numpy as jnp
from jax import lax
from jax.experimental import pallas as pl
from jax.experimental.pallas import tpu as pltpu

HEAD_DIM = 128
GRID_W = 64
GRID_SHIFT = 6
NA_ROWS = 8
NA_COLS = 16
ROPE_THETA = 10000.0
EPS = 1e-6
N_MOD = 6
NEG_INF = -1e30
LANES = 128
SUBLANES = 8

V7X_VMEM_LIMIT_BYTES = 56 * 1024 * 1024
MOD_ROWS = 8
HALO = 16
QK_ROW_CHUNKS = 8
FFN_ROW_CHUNKS = 1

f32 = jnp.float32
bf16 = jnp.bfloat16
_NT = (((1,), (1,)), ((), ()))


def _cp(*sem):
    return pltpu.CompilerParams(dimension_semantics=sem, vmem_limit_bytes=V7X_VMEM_LIMIT_BYTES)


def _tile(n, pref, unit):
    t = (min(pref, n) // unit) * unit
    while n % t:
        t -= unit
    return t


def _sigmoid(x):
    return 1.0 / (1.0 + jnp.exp(-x))


def _mod_row(i, tm, n_lat, S, B):
    return jnp.where(i * tm < n_lat, (i * tm) // S, B)


def _cast_weights_once(w_ref, wb_ref):
    @pl.when(pl.program_id(1) == 0)
    def _():
        wb_ref[...] = w_ref[...].astype(bf16)


def _mod_kernel(c_ref, w_ref, b_ref, o_ref):
    c = c_ref[...]
    sc = (c * _sigmoid(c)).astype(bf16)
    o_ref[...] = jnp.dot(sc, w_ref[...].astype(bf16), preferred_element_type=f32) + b_ref[...]


def _ada_mod(cvec, ada_w, ada_b):
    L, D, N = ada_w.shape
    tn = _tile(N, 512, LANES)
    return pl.pallas_call(
        _mod_kernel,
        out_shape=jax.ShapeDtypeStruct((L, MOD_ROWS, N), f32),
        grid=(L, N // tn),
        in_specs=[pl.BlockSpec((MOD_ROWS, D), lambda l, j: (0, 0)),
                  pl.BlockSpec((None, D, tn), lambda l, j: (l, 0, j)),
                  pl.BlockSpec((None, 1, tn), lambda l, j: (l, 0, j))],
        out_specs=pl.BlockSpec((None, MOD_ROWS, tn), lambda l, j: (l, 0, j)),
        compiler_params=_cp("arbitrary", "arbitrary"),
        name="ada_mod",
    )(cvec, ada_w, ada_b.reshape(L, 1, N))


def _rms_mod(x, g, sh, sc):
    y = x * lax.rsqrt(jnp.mean(x * x, axis=-1, keepdims=True) + EPS) * g
    return y * (1.0 + sc) + sh


def _norm_mod_kernel(x_ref, g_ref, sh_ref, sc_ref, o_ref):
    o_ref[...] = _rms_mod(x_ref[...], g_ref[...], sh_ref[...], sc_ref[...]).astype(o_ref.dtype)


def _norm_mod_halo_kernel(prev_ref, x_ref, next_ref, g_ref, sh_ref, sc_ref, o_ref, *, per):
    s = pl.program_id(1)
    sub = x_ref.shape[0]
    g, sh, sc = g_ref[...], sh_ref[...], sc_ref[...]
    o_ref[pl.ds(pl.multiple_of(HALO + s * sub, HALO), sub)] = _rms_mod(x_ref[...], g, sh, sc).astype(o_ref.dtype)

    @pl.when(s == 0)
    def _():
        o_ref[0:HALO] = _rms_mod(prev_ref[...], g, sh, sc).astype(o_ref.dtype)

    @pl.when(s == per - 1)
    def _():
        o_ref[HALO + per * sub:2 * HALO + per * sub] = _rms_mod(next_ref[...], g, sh, sc).astype(o_ref.dtype)


def _norm_kernel(x_ref, g_ref, o_ref):
    x = x_ref[...]
    o_ref[...] = x * lax.rsqrt(jnp.mean(x * x, axis=-1, keepdims=True) + EPS) * g_ref[...]


def _norm_mod(X, g, sh, sc, *, n_lat, S, B, tm):
    T, D = X.shape
    mrow = lambda i: (_mod_row(i, tm, n_lat, S, B), 0, 0)
    return pl.pallas_call(
        _norm_mod_kernel,
        out_shape=jax.ShapeDtypeStruct((T, D), bf16),
        grid=(T // tm,),
        in_specs=[pl.BlockSpec((tm, D), lambda i: (i, 0)),
                  pl.BlockSpec((1, D), lambda i: (0, 0)),
                  pl.BlockSpec((None, 1, D), mrow),
                  pl.BlockSpec((None, 1, D), mrow)],
        out_specs=pl.BlockSpec((tm, D), lambda i: (i, 0)),
        compiler_params=_cp("arbitrary"),
        name="norm_mod",
    )(X, g.reshape(1, D), sh, sc)


def _norm_mod_halo(X, g, sh, sc, *, n_lat, S, B, tm, sub):
    T, D = X.shape
    assert tm % sub == 0
    hb = tm // HALO
    n_hb = T // HALO
    per = tm // sub
    mrow = lambda i, s: (_mod_row(i, tm, n_lat, S, B), 0, 0)
    return pl.pallas_call(
        functools.partial(_norm_mod_halo_kernel, per=per),
        out_shape=jax.ShapeDtypeStruct((T // tm, tm + 2 * HALO, D), bf16),
        grid=(T // tm, per),
        in_specs=[pl.BlockSpec((HALO, D), lambda i, s: (jnp.maximum(i * hb - 1, 0), 0)),
                  pl.BlockSpec((sub, D), lambda i, s: (i * per + s, 0)),
                  pl.BlockSpec((HALO, D), lambda i, s: (jnp.minimum((i + 1) * hb, n_hb - 1), 0)),
                  pl.BlockSpec((1, D), lambda i, s: (0, 0)),
                  pl.BlockSpec((None, 1, D), mrow),
                  pl.BlockSpec((None, 1, D), mrow)],
        out_specs=pl.BlockSpec((None, tm + 2 * HALO, D), lambda i, s: (i, 0, 0)),
        compiler_params=_cp("arbitrary", "arbitrary"),
        name="norm_mod_halo",
    )(X, X, X, g.reshape(1, D), sh, sc)


def _final_norm(X, g, *, rows, tm):
    D = X.shape[1]
    return pl.pallas_call(
        _norm_kernel,
        out_shape=jax.ShapeDtypeStruct((rows, D), f32),
        grid=(rows // tm,),
        in_specs=[pl.BlockSpec((tm, D), lambda i: (i, 0)),
                  pl.BlockSpec((1, D), lambda i: (0, 0))],
        out_specs=pl.BlockSpec((tm, D), lambda i: (i, 0)),
        compiler_params=_cp("arbitrary"),
        name="final_norm",
    )(X, g.reshape(1, D))


def _mm_kernel(h_ref, w_ref, o_ref, wb_ref):
    _cast_weights_once(w_ref, wb_ref)
    o_ref[...] = jnp.dot(h_ref[...], wb_ref[...], preferred_element_type=f32).astype(o_ref.dtype)


def _proj(h, w, l, col0, ncols, out_dtype, *, tm, tn_pref, name):
    T, D = h.shape
    tn = _tile(ncols, tn_pref, LANES)
    while col0 % tn or ncols % tn:
        tn -= LANES
    c0 = col0 // tn
    return pl.pallas_call(
        _mm_kernel,
        out_shape=jax.ShapeDtypeStruct((T, ncols), out_dtype),
        grid=(ncols // tn, T // tm),
        in_specs=[pl.BlockSpec((tm, D), lambda j, i: (i, 0)),
                  pl.BlockSpec((None, D, tn), lambda j, i: (l, 0, c0 + j))],
        out_specs=pl.BlockSpec((tm, tn), lambda j, i: (i, j)),
        scratch_shapes=[pltpu.VMEM((D, tn), bf16)],
        compiler_params=_cp("arbitrary", "arbitrary"),
        name=name,
    )(h, w)


def _mm_qk_kernel(h_ref, w_ref, g_ref, cos_ref, sin_ref, o_ref, wb_ref):
    _cast_weights_once(w_ref, wb_ref)
    tm, tn = o_ref.shape
    n_chunks = QK_ROW_CHUNKS
    rc = tm // n_chunks
    even = (lax.broadcasted_iota(jnp.int32, (rc, HEAD_DIM), 1) & 1) == 0
    for r in range(n_chunks):
        rows = slice(r * rc, (r + 1) * rc)
        acc = jnp.dot(h_ref[rows, :], wb_ref[...], preferred_element_type=f32)
        cosv = cos_ref[rows, :]
        sinv = sin_ref[rows, :]
        for hh in range(tn // HEAD_DIM):
            sl = slice(hh * HEAD_DIM, (hh + 1) * HEAD_DIM)
            blk = acc[:, sl]
            y = blk * lax.rsqrt(jnp.mean(blk * blk, axis=-1, keepdims=True) + EPS) * g_ref[:, sl]
            partner = jnp.where(even, pltpu.roll(y, HEAD_DIM - 1, 1), pltpu.roll(y, 1, 1))
            o_ref[rows, sl] = (y * cosv + partner * sinv).astype(o_ref.dtype)


def _proj_qk(h, w, l, gains, cos_t, sin_t, ncols, *, tm, n_lat, S):
    T, D = h.shape
    tn = _tile(ncols, 512, LANES)
    pos_blocks = S // tm

    def pos(j, i):
        return (jnp.where(i * tm < n_lat, i % pos_blocks, pos_blocks), 0)

    return pl.pallas_call(
        _mm_qk_kernel,
        out_shape=jax.ShapeDtypeStruct((T, ncols), bf16),
        grid=(ncols // tn, T // tm),
        in_specs=[pl.BlockSpec((tm, D), lambda j, i: (i, 0)),
                  pl.BlockSpec((None, D, tn), lambda j, i: (l, 0, j)),
                  pl.BlockSpec((1, tn), lambda j, i: (0, j)),
                  pl.BlockSpec((tm, HEAD_DIM), pos),
                  pl.BlockSpec((tm, HEAD_DIM), pos)],
        out_specs=pl.BlockSpec((tm, tn), lambda j, i: (i, j)),
        scratch_shapes=[pltpu.VMEM((D, tn), bf16)],
        compiler_params=_cp("arbitrary", "arbitrary"),
        name="proj_qk",
    )(h, w, gains, cos_t, sin_t)


def _lane_max(s):
    mx = s[:, :LANES]
    for t in range(1, s.shape[1] // LANES):
        mx = jnp.maximum(mx, s[:, t * LANES:(t + 1) * LANES])
    return mx


def _attn_kernel(*refs, n_seg, group, scale):
    q_ref = refs[0]
    kv = refs[1:1 + 2 * n_seg]
    o_ref = refs[1 + 2 * n_seg]
    for g in range(group):
        sl = slice(g * HEAD_DIM, (g + 1) * HEAD_DIM)
        q = q_ref[:, sl]
        ss = [lax.dot_general(q, kv[2 * i][...], _NT, preferred_element_type=f32) * scale
              for i in range(n_seg)]
        m = jnp.max(ss[0], axis=-1, keepdims=True)
        for s in ss[1:]:
            m = jnp.maximum(m, jnp.max(s, axis=-1, keepdims=True))
        l = None
        o = None
        for i, s in enumerate(ss):
            p = jnp.exp(s - m)
            li = jnp.sum(p, axis=-1, keepdims=True)
            oi = jnp.dot(p.astype(bf16), kv[2 * i + 1][...], preferred_element_type=f32)
            l = li if l is None else l + li
            o = oi if o is None else o + oi
        o_ref[:, sl] = (o / l).astype(o_ref.dtype)


def _context_attention(q_arr, kv_seg, *, nb, heads, group, q_row0, q_len, tq, name):
    gw = group * HEAD_DIM
    n_kv = heads // group
    nq = q_len // tq
    qb0 = q_row0 // tq
    k_arr, k_col0, v_arr, v_col0, row0, length = kv_seg
    rb0, kc0, vc0 = row0 // length, k_col0 // HEAD_DIM, v_col0 // HEAD_DIM
    return pl.pallas_call(
        functools.partial(_attn_kernel, n_seg=1, group=group, scale=HEAD_DIM ** -0.5),
        out_shape=jax.ShapeDtypeStruct((nb * q_len, heads * HEAD_DIM), bf16),
        grid=(n_kv, nb, nq),
        in_specs=[pl.BlockSpec((tq, gw), lambda j, b, i: (qb0 + b * nq + i, j)),
                  pl.BlockSpec((length, HEAD_DIM), lambda j, b, i: (rb0 + b, kc0 + j)),
                  pl.BlockSpec((length, HEAD_DIM), lambda j, b, i: (rb0 + b, vc0 + j))],
        out_specs=pl.BlockSpec((tq, gw), lambda j, b, i: (b * nq + i, j)),
        compiler_params=_cp("arbitrary", "arbitrary", "arbitrary"),
        name=name,
    )(q_arr, k_arr, v_arr)


def _gqa_kernel(q_ref, kc_ref, vc_ref, kl_ref, vl_ref, o_ref,
                qs_ref, s_ref, sc_ref, m_ref, l_ref, acc_ref, *, group, kchunk, scale, unroll):
    tq = q_ref.shape[0]
    for g in range(group):
        qs_ref[g * tq:(g + 1) * tq, :] = q_ref[:, g * HEAD_DIM:(g + 1) * HEAD_DIM]
    q = qs_ref[...]
    n_chunks = kl_ref.shape[0] // kchunk
    c1 = scale * math.log2(math.e)

    s0 = lax.dot_general(q, kc_ref[...], _NT, preferred_element_type=f32) * c1
    sc_ref[...] = s0
    m_ref[...] = _lane_max(s0)

    def pass1(c, carry):
        k = kl_ref[pl.ds(pl.multiple_of(c * kchunk, kchunk), kchunk), :]
        s = lax.dot_general(q, k, _NT, preferred_element_type=f32) * c1
        s_ref[c] = s
        m_ref[...] = jnp.maximum(m_ref[...], _lane_max(s))
        return carry

    lax.fori_loop(0, n_chunks, pass1, 0, unroll=unroll)
    m_ref[...] = jnp.broadcast_to(jnp.max(m_ref[...], axis=-1, keepdims=True), m_ref.shape)

    def probs(s):
        mb = m_ref[...]
        ps = [jnp.exp2(s[:, t * LANES:(t + 1) * LANES] - mb) for t in range(s.shape[1] // LANES)]
        tot = ps[0]
        for p in ps[1:]:
            tot = tot + p
        return jnp.concatenate(ps, axis=1).astype(bf16), tot

    p0, t0 = probs(sc_ref[...])
    l_ref[...] = t0
    acc_ref[...] = jnp.dot(p0, vc_ref[...], preferred_element_type=f32)

    def pass2(c, carry):
        p, tot = probs(s_ref[c])
        l_ref[...] += tot
        v = vl_ref[pl.ds(pl.multiple_of(c * kchunk, kchunk), kchunk), :]
        acc_ref[...] += jnp.dot(p, v, preferred_element_type=f32)
        return carry

    lax.fori_loop(0, n_chunks, pass2, 0, unroll=unroll)
    o = acc_ref[...] / jnp.sum(l_ref[...], axis=-1, keepdims=True)
    for g in range(group):
        o_ref[:, g * HEAD_DIM:(g + 1) * HEAD_DIM] = o[g * tq:(g + 1) * tq].astype(o_ref.dtype)


def _gqa_latent(qk, va, *, B, S, CTX, heads, group, q_cols, out_rows, tq, kchunk):
    gw = group * HEAD_DIM
    n_kv = heads // group
    nq = S // tq
    kc0 = q_cols // HEAD_DIM
    ctx_b0 = (B * S) // CTX
    M = group * tq
    return pl.pallas_call(
        functools.partial(_gqa_kernel, group=group, kchunk=kchunk, scale=HEAD_DIM ** -0.5, unroll=4),
        out_shape=jax.ShapeDtypeStruct((out_rows, q_cols), bf16),
        grid=(n_kv, B, nq),
        in_specs=[pl.BlockSpec((tq, gw), lambda j, b, i: (b * nq + i, j)),
                  pl.BlockSpec((CTX, HEAD_DIM), lambda j, b, i: (ctx_b0 + b, kc0 + j)),
                  pl.BlockSpec((CTX, HEAD_DIM), lambda j, b, i: (ctx_b0 + b, j)),
                  pl.BlockSpec((S, HEAD_DIM), lambda j, b, i: (b, kc0 + j)),
                  pl.BlockSpec((S, HEAD_DIM), lambda j, b, i: (b, j))],
        out_specs=pl.BlockSpec((tq, gw), lambda j, b, i: (b * nq + i, j)),
        scratch_shapes=[pltpu.VMEM((M, HEAD_DIM), bf16),
                        pltpu.VMEM((S // kchunk, M, kchunk), f32),
                        pltpu.VMEM((M, CTX), f32),
                        pltpu.VMEM((M, LANES), f32),
                        pltpu.VMEM((M, LANES), f32),
                        pltpu.VMEM((M, HEAD_DIM), f32)],
        compiler_params=_cp("arbitrary", "arbitrary", "arbitrary"),
        name="gqa_latent",
    )(qk, qk, va, qk, va)


def _na_bias_kernel(rpb_ref, o_ref, tw_ref, *, rows):
    h = pl.program_id(0)
    case = pl.program_id(1)
    n_r = 2 * NA_ROWS - 1
    n_c = 2 * NA_COLS - 1
    per_head = n_r * n_c
    shape = (GRID_W, 2 * GRID_W)
    qc = lax.broadcasted_iota(jnp.int32, shape, 0)
    lane = lax.broadcasted_iota(jnp.int32, shape, 1)
    kc = lane & (GRID_W - 1)
    second = lane >= GRID_W
    dc = kc - qc
    cs = jnp.clip(qc - NA_COLS // 2, 0, GRID_W - NA_COLS)
    col_ok = (kc >= cs) & (kc < cs + NA_COLS)
    neg = jnp.full(shape, NEG_INF, f32)
    lo, hi = -NA_ROWS, NA_ROWS - 1
    span = 2 * NA_ROWS - 1
    for a in range(-span, span):
        if lo <= a <= hi:
            acc = neg
            for bb in range(n_c):
                s0 = rpb_ref[h * per_head + (a + NA_ROWS - 1) * n_c + bb] if -NA_ROWS < a < NA_ROWS else NEG_INF
                s1 = rpb_ref[h * per_head + (a + NA_ROWS) * n_c + bb] if -NA_ROWS < a + 1 < NA_ROWS else NEG_INF
                acc = jnp.where(dc == bb - (NA_COLS - 1), jnp.where(second, s1, s0), acc)
            tw_ref[a + span] = jnp.where(col_ok, acc, NEG_INF)
        else:
            tw_ref[a + span] = neg
    qbase = jnp.where(case == 0, 0, jnp.where(case == 1, NA_ROWS, rows - NA_ROWS))
    kbase = jnp.where(case == 0, 0, jnp.where(case == 1, NA_ROWS // 2, rows - 2 * NA_ROWS))
    off = kbase - qbase
    pair_w = 2 * GRID_W
    chunk_w = 2 * pair_w
    for qr in range(NA_ROWS):
        for kp in range(NA_ROWS):
            a = 2 * kp - qr + off
            lane0 = (kp % 2) * pair_w
            o_ref[kp // 2, qr * GRID_W:(qr + 1) * GRID_W, lane0:lane0 + pair_w] = tw_ref[a + span]
    full = (NA_ROWS * GRID_W, chunk_w)
    qrow = qbase + (lax.broadcasted_iota(jnp.int32, full, 0) >> GRID_SHIFT)
    rs = jnp.clip(qrow - NA_ROWS // 2, 0, rows - NA_ROWS)
    for ch in range(4):
        krow = kbase + 4 * ch + (lax.broadcasted_iota(jnp.int32, full, 1) >> GRID_SHIFT)
        valid = (krow >= rs) & (krow < rs + NA_ROWS)
        o_ref[ch] = jnp.where(valid, o_ref[ch], NEG_INF)


def _na_bias(rpb_l, *, rows):
    H = rpb_l.shape[0]
    tq, kb = NA_ROWS * GRID_W, 4 * GRID_W
    return pl.pallas_call(
        functools.partial(_na_bias_kernel, rows=rows),
        out_shape=jax.ShapeDtypeStruct((H, 3, 4, tq, kb), f32),
        grid=(H, 3),
        in_specs=[pl.BlockSpec(memory_space=pltpu.SMEM)],
        out_specs=pl.BlockSpec((None, None, 4, tq, kb), lambda h, c: (h, c, 0, 0, 0)),
        scratch_shapes=[pltpu.VMEM((2 * (2 * NA_ROWS - 1), GRID_W, 2 * GRID_W), f32)],
        compiler_params=_cp("arbitrary", "arbitrary"),
        name="na_bias",
    )(rpb_l.reshape(-1))


def _na_kernel(q_ref, k0, k1, k2, k3, v0, v1, v2, v3, kc_ref, vc_ref, bias_ref, o_ref,
               s_ref, sc_ref, m_ref, l_ref, acc_ref, *, scale, hp):
    ks = (k0, k1, k2, k3)
    vs = (v0, v1, v2, v3)
    for hh in range(hp):
        sl = slice(hh * HEAD_DIM, (hh + 1) * HEAD_DIM)
        q = q_ref[:, sl]
        s0 = lax.dot_general(q, kc_ref[:, sl], _NT, preferred_element_type=f32) * scale
        sc_ref[hh] = s0
        m_ref[hh] = _lane_max(s0)
        for i in range(4):
            s = lax.dot_general(q, ks[i][:, sl], _NT, preferred_element_type=f32) * scale + bias_ref[hh, i]
            s_ref[hh, i] = s
            m_ref[hh] = jnp.maximum(m_ref[hh], _lane_max(s))
        m_ref[hh] = jnp.broadcast_to(jnp.max(m_ref[hh], axis=-1, keepdims=True), m_ref.shape[1:])

        def probs(s):
            mb = m_ref[hh]
            ps = [jnp.exp(s[:, t * LANES:(t + 1) * LANES] - mb) for t in range(s.shape[1] // LANES)]
            tot = ps[0]
            for p in ps[1:]:
                tot = tot + p
            return jnp.concatenate(ps, axis=1).astype(bf16), tot

        p0, t0 = probs(sc_ref[hh])
        l_ref[hh] = t0
        acc_ref[hh] = jnp.dot(p0, vc_ref[:, sl], preferred_element_type=f32)
        for i in range(4):
            p, tot = probs(s_ref[hh, i])
            l_ref[hh] += tot
            acc_ref[hh] += jnp.dot(p, vs[i][:, sl], preferred_element_type=f32)
        o_ref[:, sl] = (acc_ref[hh] / jnp.sum(l_ref[hh], axis=-1, keepdims=True)).astype(o_ref.dtype)


def _na_attention(nqkv, bias, *, B, S, CTX, heads, out_rows):
    rows = S // GRID_W
    tq = NA_ROWS * GRID_W
    kb = tq // 2
    nt = rows // NA_ROWS
    kb_per_seq = S // kb
    ctx_b0 = (B * S) // CTX

    def case(t):
        return jnp.where(t == 0, 0, jnp.where(t == nt - 1, 2, 1))

    def kblock(t):
        return jnp.clip(2 * t - 1, 0, kb_per_seq - 4)

    hp = next(n for n in (6, 4, 3, 2, 1) if heads % n == 0)
    hw = hp * HEAD_DIM
    hblocks = heads // hp

    def kspec(col0, jj):
        return pl.BlockSpec((kb, hw), lambda h, t, b: (b * kb_per_seq + kblock(t) + jj, col0 + h))

    in_specs = [pl.BlockSpec((tq, hw), lambda h, t, b: (b * nt + t, h))]
    in_specs += [kspec(hblocks, jj) for jj in range(4)]
    in_specs += [kspec(2 * hblocks, jj) for jj in range(4)]
    in_specs += [pl.BlockSpec((CTX, hw), lambda h, t, b: (ctx_b0 + b, hblocks + h)),
                 pl.BlockSpec((CTX, hw), lambda h, t, b: (ctx_b0 + b, 2 * hblocks + h)),
                 pl.BlockSpec((hp, None, 4, tq, kb), lambda h, t, b: (h, case(t), 0, 0, 0))]
    return pl.pallas_call(
        functools.partial(_na_kernel, scale=HEAD_DIM ** -0.5, hp=hp),
        out_shape=jax.ShapeDtypeStruct((out_rows, heads * HEAD_DIM), bf16),
        grid=(hblocks, nt, B),
        in_specs=in_specs,
        out_specs=pl.BlockSpec((tq, hw), lambda h, t, b: (b * nt + t, h)),
        scratch_shapes=[pltpu.VMEM((hp, 4, tq, kb), f32),
                        pltpu.VMEM((hp, tq, CTX), f32),
                        pltpu.VMEM((hp, tq, LANES), f32),
                        pltpu.VMEM((hp, tq, LANES), f32),
                        pltpu.VMEM((hp, tq, HEAD_DIM), f32)],
        compiler_params=_cp("arbitrary", "arbitrary", "arbitrary"),
        name="na_attn",
    )(*([nqkv] * 11), bias)


def _conf_kernel(prev_ref, cur_ref, next_ref, dww_ref, dwb_ref, lng_ref, lnb_ref, pww_ref, pwb_ref,
                 o_ref, z_ref, zs_ref, y_ref, pwb16_ref, *, tm, ch, n_taps, lat_tiles, lat_tiles_per_seq,
                 ctx_tiles_per_seq):
    i = pl.program_id(0)

    @pl.when(i == 0)
    def _():
        pwb16_ref[...] = pww_ref[...].astype(bf16)

    is_lat = i < lat_tiles
    per_seq = jnp.where(is_lat, lat_tiles_per_seq, ctx_tiles_per_seq)
    local = jnp.where(is_lat, i, i - lat_tiles) % per_seq
    first = local == 0
    last = local == per_seq - 1

    def glu(u):
        return u[:, :ch] * _sigmoid(u[:, ch:])

    z_ref[0:HALO] = jnp.where(first, 0.0, glu(prev_ref[...]))
    z_ref[HALO:HALO + tm] = glu(cur_ref[...])
    z_ref[HALO + tm:2 * HALO + tm] = jnp.where(last, 0.0, glu(next_ref[...]))
    half = (n_taps - 1) // 2
    n_sh = tm + 2 * HALO - SUBLANES
    for b in range(1, SUBLANES):
        zs_ref[b - 1, 0:n_sh, :] = z_ref[b:b + n_sh, :]
    rb = 128
    for cb in range(ch // LANES):
        cols = slice(cb * LANES, (cb + 1) * LANES)
        for r in range(tm // rb):
            acc = jnp.zeros((rb, LANES), f32)
            for k in range(n_taps):
                a, b = divmod(HALO - half + k, SUBLANES)
                r0 = r * rb + a * SUBLANES
                win = z_ref[r0:r0 + rb, cols] if b == 0 else zs_ref[b - 1, r0:r0 + rb, cols]
                acc = acc + dww_ref[k:k + 1, cols] * win
            y_ref[r * rb:(r + 1) * rb, cols] = acc + dwb_ref[:, cols]
    y = y_ref[...]
    mu = jnp.mean(y, axis=-1, keepdims=True)
    d = y - mu
    var = jnp.mean(d * d, axis=-1, keepdims=True)
    yn = d * lax.rsqrt(var + EPS) * lng_ref[...] + lnb_ref[...]
    act = (yn * _sigmoid(yn)).astype(bf16)
    o_ref[...] = (jnp.dot(act, pwb16_ref[...], preferred_element_type=f32) + pwb_ref[...]).astype(o_ref.dtype)


def _conformer(bu, dww, dwb, lng, lnb, pww, pwb, *, n_rows, n_lat, S, CTX, tm):
    T = bu.shape[0]
    ch = bu.shape[1] // 2
    n_taps = dww.shape[0]
    hb = tm // HALO
    n_hb = T // HALO
    kern = functools.partial(_conf_kernel, tm=tm, ch=ch, n_taps=n_taps, lat_tiles=n_lat // tm,
                             lat_tiles_per_seq=S // tm, ctx_tiles_per_seq=CTX // tm)
    row = lambda i: (0, 0)
    return pl.pallas_call(
        kern,
        out_shape=jax.ShapeDtypeStruct((n_rows, ch), bf16),
        grid=(n_rows // tm,),
        in_specs=[pl.BlockSpec((HALO, 2 * ch), lambda i: (jnp.maximum(i * hb - 1, 0), 0)),
                  pl.BlockSpec((tm, 2 * ch), lambda i: (i, 0)),
                  pl.BlockSpec((HALO, 2 * ch), lambda i: (jnp.minimum((i + 1) * hb, n_hb - 1), 0)),
                  pl.BlockSpec((n_taps, ch), row),
                  pl.BlockSpec((1, ch), row), pl.BlockSpec((1, ch), row), pl.BlockSpec((1, ch), row),
                  pl.BlockSpec((ch, ch), row), pl.BlockSpec((1, ch), row)],
        out_specs=pl.BlockSpec((tm, ch), lambda i: (i, 0)),
        scratch_shapes=[pltpu.VMEM((tm + 2 * HALO, ch), f32),
                        pltpu.VMEM((SUBLANES - 1, tm + 2 * HALO, ch), f32), pltpu.VMEM((tm, ch), f32),
                        pltpu.VMEM((ch, ch), bf16)],
        compiler_params=_cp("arbitrary"),
        name="conformer",
    )(bu, bu, bu, dww, dwb.reshape(1, ch), lng.reshape(1, ch), lnb.reshape(1, ch), pww, pwb.reshape(1, ch))


def _mm_res_kernel(*refs, splits):
    n = len(splits)
    w_ref, x_ref, g_ref, o_ref, wb_ref = refs[n:]
    _cast_weights_once(w_ref, wb_ref)
    acc = None
    off = 0
    for r, k in zip(refs[:n], splits):
        part = jnp.dot(r[...], wb_ref[off:off + k, :], preferred_element_type=f32)
        acc = part if acc is None else acc + part
        off += k
    o_ref[...] = x_ref[...] + g_ref[...] * acc


def _proj_residual(lhs, w, l, X, gate, *, n_rows, n_lat, S, B, tm, tn, name):
    _, K, N = w.shape
    splits = tuple(a.shape[1] for a in lhs)
    in_specs = [pl.BlockSpec((tm, k), lambda j, i: (i, 0)) for k in splits]
    in_specs += [pl.BlockSpec((None, K, tn), lambda j, i: (l, 0, j)),
                 pl.BlockSpec((tm, tn), lambda j, i: (i, j)),
                 pl.BlockSpec((None, 1, tn), lambda j, i: (_mod_row(i, tm, n_lat, S, B), 0, j))]
    return pl.pallas_call(
        functools.partial(_mm_res_kernel, splits=splits),
        out_shape=jax.ShapeDtypeStruct((n_rows, N), f32),
        grid=(N // tn, n_rows // tm),
        in_specs=in_specs,
        out_specs=pl.BlockSpec((tm, tn), lambda j, i: (i, j)),
        scratch_shapes=[pltpu.VMEM((K, tn), bf16)],
        compiler_params=_cp("arbitrary", "arbitrary"),
        name=name,
    )(*lhs, w, X, gate)


def _ffn_up_kernel(h_ref, wg_ref, wv_ref, cwg_ref, cwv_ref, cbg_ref, cbv_ref, mp_ref, mn_ref, o_ref,
                   wgb_ref, wvb_ref, *, tm):
    _cast_weights_once(wg_ref, wgb_ref)
    _cast_weights_once(wv_ref, wvb_ref)
    tf = o_ref.shape[1]
    rc = tm // FFN_ROW_CHUNKS
    n_ext = rc + 2 * HALO
    for r in range(FFN_ROW_CHUNKS):
        rows = slice(r * rc, (r + 1) * rc)
        h = h_ref[r * rc:r * rc + n_ext, :]
        has_prev = jnp.concatenate([mp_ref[rows, :]] * (tf // LANES), axis=1)
        has_next = jnp.concatenate([mn_ref[rows, :]] * (tf // LANES), axis=1)

        def branch(wb_ref, cw_ref, cb_ref):
            u = jnp.dot(h, wb_ref[...], preferred_element_type=f32)
            up = pltpu.roll(u, 1, 0)[HALO:HALO + rc]
            dn = pltpu.roll(u, n_ext - 1, 0)[HALO:HALO + rc]
            return (cw_ref[0:1, :] * (up * has_prev) + cw_ref[1:2, :] * u[HALO:HALO + rc]
                    + cw_ref[2:3, :] * (dn * has_next) + cb_ref[...])

        gate = branch(wgb_ref, cwg_ref, cbg_ref)
        val = branch(wvb_ref, cwv_ref, cbv_ref)
        o_ref[rows, :] = (gate * _sigmoid(gate) * val).astype(o_ref.dtype)


def _ffn_up(hext, w_up, l, cw, cb, has_prev, has_next, *, n_rows, tm, tf):
    D = hext.shape[2]
    F = w_up.shape[2] // 2
    nf = F // tf
    cb2 = cb.reshape(1, 2 * F)
    return pl.pallas_call(
        functools.partial(_ffn_up_kernel, tm=tm),
        out_shape=jax.ShapeDtypeStruct((n_rows, F), bf16),
        grid=(nf, n_rows // tm),
        in_specs=[pl.BlockSpec((None, tm + 2 * HALO, D), lambda j, i: (i, 0, 0)),
                  pl.BlockSpec((None, D, tf), lambda j, i: (l, 0, j)),
                  pl.BlockSpec((None, D, tf), lambda j, i: (l, 0, nf + j)),
                  pl.BlockSpec((3, tf), lambda j, i: (0, j)),
                  pl.BlockSpec((3, tf), lambda j, i: (0, nf + j)),
                  pl.BlockSpec((1, tf), lambda j, i: (0, j)),
                  pl.BlockSpec((1, tf), lambda j, i: (0, nf + j)),
                  pl.BlockSpec((tm, LANES), lambda j, i: (i, 0)),
                  pl.BlockSpec((tm, LANES), lambda j, i: (i, 0))],
        out_specs=pl.BlockSpec((tm, tf), lambda j, i: (i, j)),
        scratch_shapes=[pltpu.VMEM((D, tf), bf16), pltpu.VMEM((D, tf), bf16)],
        compiler_params=_cp("arbitrary", "arbitrary"),
        name="ffn_up",
    )(hext, w_up, w_up, cw, cw, cb2, cb2, has_prev, has_next)


def _sequence_edge_masks(n_lat, n_ctx, S, CTX):
    row = jnp.arange(n_lat + n_ctx, dtype=jnp.int32)
    is_lat = row < n_lat
    pos = jnp.where(is_lat, row % S, (row - n_lat) % CTX)
    last = jnp.where(is_lat, S - 1, CTX - 1)
    has_prev = jnp.broadcast_to((pos != 0).astype(f32)[:, None], (n_lat + n_ctx, LANES))
    has_next = jnp.broadcast_to((pos != last).astype(f32)[:, None], (n_lat + n_ctx, LANES))
    return has_prev, has_next


def _rope_tables(S, tm):
    t = jnp.arange(S, dtype=jnp.int32)
    row = (t // GRID_W).astype(f32)
    col = (t % GRID_W).astype(f32)
    half = HEAD_DIM // 2
    inv = ROPE_THETA ** (-jnp.arange(0, half, 2, dtype=f32) / half)
    ang = jnp.concatenate([row[:, None] * inv, col[:, None] * inv], axis=-1)
    cos = jnp.repeat(jnp.cos(ang), 2, axis=-1)
    sin = jnp.repeat(jnp.sin(ang), 2, axis=-1)
    sign = jnp.where(jnp.arange(HEAD_DIM) % 2 == 0, -1.0, 1.0).astype(f32)
    cos_t = jnp.concatenate([cos, jnp.ones((tm, HEAD_DIM), f32)], axis=0)
    sin_t = jnp.concatenate([sin * sign, jnp.zeros((tm, HEAD_DIM), f32)], axis=0)
    return cos_t, sin_t


def kernel(x, c, ctx, c_ctx, ada_w, ada_b, norm1_g, norm2_g, w_in, a_qn_g, a_kn_g, b_dw_w, b_dw_b,
           b_ln_g, b_ln_b, b_pw_w, b_pw_b, c_rpb, w_out, ffn_w_up, ffn_dw_w, ffn_dw_b, ffn_w_down, final_g):
    B, S, D = x.shape
    CTX = ctx.shape[1]
    L = ada_w.shape[0]
    n_slots = D // HEAD_DIM
    a_heads = 3 * n_slots // 8
    a_kv = a_heads // 3
    c_heads = 3 * n_slots // 8
    b_ch = D - (a_heads + c_heads) * HEAD_DIM
    a_q, a_kvw, c_w = a_heads * HEAD_DIM, a_kv * HEAD_DIM, c_heads * HEAD_DIM
    F = ffn_w_down.shape[1]
    n_lat, n_ctx = B * S, B * CTX
    T = n_lat + n_ctx
    rows = S // GRID_W
    assert B + 1 <= MOD_ROWS and rows % NA_ROWS == 0 and rows >= 2 * NA_ROWS
    assert S % CTX == 0 and n_lat % (NA_ROWS * GRID_W) == 0
    assert S & (S - 1) == 0 and CTX & (CTX - 1) == 0

    tm = _tile(n_ctx, 1024, CTX)
    assert n_lat % tm == 0 and S % tm == 0
    tm_down = _tile(tm, 512, CTX)
    tm_conv = _tile(CTX, 256, LANES)
    tm_norm = _tile(tm, 512, CTX)
    tn = _tile(D, 512, LANES)
    tf = _tile(F, 256, LANES)
    tq = _tile(S, 512, LANES)
    kchunk = _tile(S, 512, LANES)
    tq_ctx = _tile(CTX, 256, LANES)

    X = jnp.concatenate([x.reshape(n_lat, D), ctx.reshape(n_ctx, D)], axis=0)
    cvec = jnp.concatenate([c, c_ctx[None, :], jnp.zeros((MOD_ROWS - B - 1, D), f32)], axis=0)
    mod = _ada_mod(cvec, ada_w, ada_b)
    cos_t, sin_t = _rope_tables(S, tm)
    has_prev, has_next = _sequence_edge_masks(n_lat, n_ctx, S, CTX)
    qk_w = a_q + a_kvw
    col_v, col_bu, col_n = qk_w, qk_w + a_kvw, qk_w + a_kvw + 2 * b_ch

    for l in range(L):
        ctx_needed = l < L - 1
        n_rows = T if ctx_needed else n_lat
        m = mod[l].reshape(MOD_ROWS, N_MOD, 1, D)
        sh1, sc1, g1, sh2, sc2, g2 = (m[:, k] for k in range(N_MOD))
        common = dict(n_lat=n_lat, S=S, B=B)

        h = _norm_mod(X, norm1_g[l], sh1, sc1, tm=tm_norm, **common)
        gains = jnp.concatenate([jnp.tile(a_qn_g[l], a_heads), jnp.tile(a_kn_g[l], a_kv)]).reshape(1, qk_w)
        qk = _proj_qk(h, w_in, l, gains, cos_t, sin_t, qk_w, tm=tm, n_lat=n_lat, S=S)
        va = _proj(h, w_in, l, col_v, a_kvw, bf16, tm=tm, tn_pref=512, name="proj_v")
        bu = _proj(h, w_in, l, col_bu, 2 * b_ch, f32, tm=tm, tn_pref=512, name="proj_bu")
        nqkv = _proj(h, w_in, l, col_n, 3 * c_w, bf16, tm=tm, tn_pref=512, name="proj_n")

        o_a = _gqa_latent(qk, va, B=B, S=S, CTX=CTX, heads=a_heads, group=a_heads // a_kv, q_cols=a_q,
                          out_rows=n_lat, tq=tq, kchunk=kchunk)
        bias = _na_bias(c_rpb[l], rows=rows)
        o_c = _na_attention(nqkv, bias, B=B, S=S, CTX=CTX, heads=c_heads, out_rows=n_lat)
        if ctx_needed:
            oc_a = _context_attention(qk, (qk, a_q, va, 0, n_lat, CTX), nb=B, heads=a_heads,
                                      group=a_heads // a_kv, q_row0=n_lat, q_len=CTX, tq=tq_ctx,
                                      name="gqa_context")
            oc_c = _context_attention(nqkv, (nqkv, c_w, nqkv, 2 * c_w, n_lat, CTX), nb=B, heads=c_heads,
                                      group=1, q_row0=n_lat, q_len=CTX, tq=tq_ctx, name="mha_context")
            o_a = jnp.concatenate([o_a, oc_a], axis=0)
            o_c = jnp.concatenate([o_c, oc_c], axis=0)
        o_b = _conformer(bu, b_dw_w[l], b_dw_b[l], b_ln_g[l], b_ln_b[l], b_pw_w[l], b_pw_b[l],
                         n_rows=n_rows, n_lat=n_lat, S=S, CTX=CTX, tm=tm_conv)
        X = _proj_residual([o_a, o_b, o_c], w_out, l, X, g1, n_rows=n_rows, tm=tm, tn=tn, name="proj_out",
                           **common)

        h2 = _norm_mod_halo(X, norm2_g[l], sh2, sc2, tm=tm, sub=tm_norm, **common)
        act = _ffn_up(h2, ffn_w_up, l, ffn_dw_w[l], ffn_dw_b[l], has_prev, has_next, n_rows=n_rows,
                      tm=tm, tf=tf)
        X = _proj_residual([act], ffn_w_down, l, X, g2, n_rows=n_rows, tm=tm_down, tn=tn, name="ffn_down",
                           **common)

    out = _final_norm(X, final_g, rows=n_lat, tm=_tile(n_lat, 512, SUBLANES))
    return out.reshape(B, S, D)
```
